```python
import jax, jax.numpy as jnp
from jax import lax
import numpy as np

D_MODEL = 2048
BATCH = 8
SEQ = 2048
DEPTH = 1

D_CONV = D_MODEL // 2
CONV_GROUPS = 16
CONV_WIDTH = 3
D_ATTN = D_MODEL - D_CONV
N_HEADS = 8
HEAD_DIM = D_ATTN // N_HEADS
D_MIX = D_CONV + D_ATTN
D_IN = 4 * D_CONV + 4 * D_ATTN
MOBA_BLOCK = 256
MOBA_TOPK = 3
Q_CHUNK = 64
EPS = 1e-6

kernel_name = "hymba_conv_moba_hybrid"


def rms_norm(x, gain):
    xf = x.astype(jnp.float32)
    y = xf * lax.rsqrt(jnp.mean(xf * xf, axis=-1, keepdims=True) + EPS)
    return (y * gain.astype(jnp.float32)).astype(x.dtype)


def short_gated_conv(h, b_gate, c_gate, conv_w):
    seq = h.shape[1]
    u = c_gate * h
    u_pad = jnp.pad(u, ((0, 0), (CONV_WIDTH - 1, 0), (0, 0)))
    conv = u_pad[:, 0:seq] * conv_w[0]
    for tap in range(1, CONV_WIDTH):
        conv = conv + u_pad[:, tap:tap + seq] * conv_w[tap]
    return b_gate * conv


def moba_attention(q, k, v):
    bsz, n_heads, seq, hd = q.shape
    n_blocks = -(-seq // MOBA_BLOCK)
    pad = n_blocks * MOBA_BLOCK - seq
    k_pad = jnp.pad(k, ((0, 0), (0, 0), (0, pad), (0, 0)))
    v_pad = jnp.pad(v, ((0, 0), (0, 0), (0, pad), (0, 0)))
    k_blocks = k_pad.reshape(bsz, n_heads, n_blocks, MOBA_BLOCK, hd)
    v_blocks = v_pad.reshape(bsz, n_heads, n_blocks, MOBA_BLOCK, hd)
    k_mean = jnp.mean(k_blocks.astype(jnp.float32), axis=3)
    k_sel = min(MOBA_TOPK, n_blocks)
    scale = hd ** -0.5
    n_chunks = seq // Q_CHUNK
    q_chunks = q.reshape(bsz, n_heads, n_chunks, Q_CHUNK, hd).transpose(2, 0, 1, 3, 4)
    b_idx = jnp.arange(bsz)[:, None, None]
    h_idx = jnp.arange(n_heads)[None, :, None]
    block_ids = jnp.arange(n_blocks)
    slot_ids = jnp.arange(k_sel)
    key_offsets = jnp.arange(MOBA_BLOCK)

    def chunk_fn(args):
        ci, q_blk = args
        start = ci * Q_CHUNK
        own = start // MOBA_BLOCK
        q_pos = start + jnp.arange(Q_CHUNK)
        gate = jnp.einsum('bhqd,bhnd->bhqn', q_blk.astype(jnp.float32), k_mean)
        gate = jnp.where(block_ids < own, gate, -jnp.inf)
        _, sel = lax.top_k(gate, k_sel)
        sel_valid = slot_ids < own
        k_own = lax.dynamic_index_in_dim(k_blocks, own, axis=2, keepdims=False)
        v_own = lax.dynamic_index_in_dim(v_blocks, own, axis=2, keepdims=False)
        s_own = jnp.einsum('bhqd,bhkd->bhqk', q_blk, k_own).astype(jnp.float32) * scale
        k_pos = own * MOBA_BLOCK + key_offsets
        s_own = jnp.where(k_pos[None, :] <= q_pos[:, None], s_own, -jnp.inf)
        scores = []
        for slot in range(k_sel):
            k_g = k_blocks[b_idx, h_idx, sel[..., slot]]
            s = jnp.einsum('bhqd,bhqkd->bhqk', q_blk, k_g).astype(jnp.float32) * scale
            scores.append(jnp.where(sel_valid[slot], s, -jnp.inf))
        scores.append(s_own)
        probs = jax.nn.softmax(jnp.concatenate(scores, axis=-1), axis=-1).astype(v.dtype)
        out = jnp.einsum('bhqk,bhkd->bhqd', probs[..., k_sel * MOBA_BLOCK:], v_own)
        for slot in range(k_sel):
            v_g = v_blocks[b_idx, h_idx, sel[..., slot]]
            p_slot = probs[..., slot * MOBA_BLOCK:(slot + 1) * MOBA_BLOCK]
            out = out + jnp.einsum('bhqk,bhqkd->bhqd', p_slot, v_g)
        return out.astype(q.dtype)

    outs = lax.map(chunk_fn, (jnp.arange(n_chunks), q_chunks))
    return outs.transpose(1, 2, 0, 3, 4).reshape(bsz, n_heads, seq, hd)


def setup_inputs(seed: int = 0) -> dict:
    key = jax.random.key(seed)
    ks = jax.random.split(key, 9)
    x = jax.random.normal(ks[0], (BATCH, SEQ, D_MODEL), jnp.float32)
    norm_gain = 1.0 + 0.1 * jax.random.normal(ks[1], (DEPTH, D_MODEL), jnp.float32)
    w_in = jax.random.normal(ks[2], (DEPTH, D_MODEL, D_IN), jnp.float32) * D_MODEL ** -0.5
    conv_w = jax.random.normal(ks[3], (DEPTH, CONV_WIDTH, D_CONV), jnp.float32) * CONV_WIDTH ** -0.5
    q_norm_gain = 1.0 + 0.1 * jax.random.normal(ks[4], (DEPTH, HEAD_DIM), jnp.float32)
    k_norm_gain = 1.0 + 0.1 * jax.random.normal(ks[5], (DEPTH, HEAD_DIM), jnp.float32)
    conv_out_gain = 1.0 + 0.1 * jax.random.normal(ks[6], (DEPTH, D_CONV), jnp.float32)
    attn_out_gain = 1.0 + 0.1 * jax.random.normal(ks[7], (DEPTH, D_ATTN), jnp.float32)
    w_out = jax.random.normal(ks[8], (DEPTH, D_MIX, D_MODEL), jnp.float32) * D_MIX ** -0.5
    return {"x": x, "norm_gain": norm_gain, "w_in": w_in, "conv_w": conv_w,
            "q_norm_gain": q_norm_gain, "k_norm_gain": k_norm_gain,
            "conv_out_gain": conv_out_gain, "attn_out_gain": attn_out_gain, "w_out": w_out}


def reference(x, norm_gain, w_in, conv_w, q_norm_gain, k_norm_gain, conv_out_gain, attn_out_gain, w_out):
    bsz, seq, _ = x.shape
    split_points = [D_CONV, 2 * D_CONV, 3 * D_CONV, 4 * D_CONV,
                    4 * D_CONV + D_ATTN, 4 * D_CONV + 2 * D_ATTN, 4 * D_CONV + 3 * D_ATTN]
    for layer in range(DEPTH):
        h = rms_norm(x, norm_gain[layer])
        proj = jnp.einsum('bsd,de->bse', h, w_in[layer])
        h_conv, b_gate, c_gate, z_conv, q, k, v, z_attn = jnp.split(proj, split_points, axis=-1)
        y_conv = short_gated_conv(h_conv, b_gate, c_gate, conv_w[layer])
        y_conv = rms_norm(y_conv, conv_out_gain[layer]) * jax.nn.silu(z_conv)
        q = rms_norm(q.reshape(bsz, seq, N_HEADS, HEAD_DIM), q_norm_gain[layer]).transpose(0, 2, 1, 3)
        k = rms_norm(k.reshape(bsz, seq, N_HEADS, HEAD_DIM), k_norm_gain[layer]).transpose(0, 2, 1, 3)
        v = v.reshape(bsz, seq, N_HEADS, HEAD_DIM).transpose(0, 2, 1, 3)
        y_attn = moba_attention(q, k, v).transpose(0, 2, 1, 3).reshape(bsz, seq, D_ATTN)
        y_attn = rms_norm(y_attn, attn_out_gain[layer]) * jax.nn.silu(z_attn)
        y = jnp.concatenate([y_conv, y_attn], axis=-1)
        x = x + jnp.einsum('bse,ed->bsd', y, w_out[layer])
    return x
```

```python
import functools
import math

import jax
import jax.numpy as jnp
from jax import lax
from jax.experimental import pallas as pl
from jax.experimental.pallas import tpu as pltpu

N_HEADS = 8
CONV_WIDTH = 3
MOBA_BLOCK = 256
MOBA_TOPK = 3
EPS = 1e-6
N_SEG = 8
SEG_H, SEG_B, SEG_C, SEG_ZC, SEG_Q, SEG_K, SEG_V, SEG_ZA = range(N_SEG)

MASKED = -1e30
LOG2E = math.log2(math.e)

V7X_VMEM_LIMIT_BYTES = 56 * 1024 * 1024
SUBLANES = 8

IN_TILE_M = 512
OUT_TILE_M = 512


def _rms_norm_f32(x, gain):
    return x * lax.rsqrt(jnp.mean(x * x, axis=-1, keepdims=True) + EPS) * gain


def _inproj_kernel(x_ref, g_ref, w_ref, o_ref, h_ref):
    @pl.when(pl.program_id(1) == 0)
    def _():
        h_ref[...] = _rms_norm_f32(x_ref[...], g_ref[...]).astype(h_ref.dtype)

    acc = jnp.dot(h_ref[...], w_ref[...], preferred_element_type=jnp.float32)
    o_ref[0] = acc.astype(o_ref.dtype)


def _inproj(x2d, gain, w_bf16):
    m, d = x2d.shape
    seg = w_bf16.shape[1] // N_SEG
    tm = IN_TILE_M
    return pl.pallas_call(
        _inproj_kernel,
        grid=(m // tm, N_SEG),
        in_specs=[
            pl.BlockSpec((tm, d), lambda i, j: (i, 0)),
            pl.BlockSpec((1, d), lambda i, j: (0, 0)),
            pl.BlockSpec((d, seg), lambda i, j: (0, j)),
        ],
        out_specs=pl.BlockSpec((1, tm, seg), lambda i, j: (j, i, 0)),
        out_shape=jax.ShapeDtypeStruct((N_SEG, m, seg), jnp.bfloat16),
        scratch_shapes=[pltpu.VMEM((tm, d), jnp.bfloat16)],
        compiler_params=pltpu.CompilerParams(
            dimension_semantics=("arbitrary", "arbitrary"),
            vmem_limit_bytes=V7X_VMEM_LIMIT_BYTES,
        ),
        name="inproj",
    )(x2d, gain.reshape(1, d), w_bf16)


def _moba_kernel(q_ref, k_ref, v_ref, qg_ref, kg_ref, o_ref, kn_ref, vt_ref):
    seq, hd = k_ref.shape[2], k_ref.shape[3]
    blk = MOBA_BLOCK
    n_blocks = seq // blk
    contract_last = (((1,), (1,)), ((), ()))

    kn = _rms_norm_f32(k_ref[0, 0].astype(jnp.float32), kg_ref[...])
    kn_ref[...] = kn.astype(kn_ref.dtype)
    k_mean = jnp.mean(kn.reshape(n_blocks, blk, hd), axis=1)
    km_hi = k_mean.astype(jnp.bfloat16)
    km_lo = (k_mean - km_hi.astype(jnp.float32)).astype(jnp.bfloat16)
    km_hl = jnp.concatenate([km_hi, km_lo], axis=0)
    vt_ref[...] = v_ref[0, 0].astype(jnp.float32).T.astype(vt_ref.dtype)

    key_row = lax.broadcasted_iota(jnp.int32, (blk, blk), 0)
    qry_col = lax.broadcasted_iota(jnp.int32, (blk, blk), 1)
    causal = key_row <= qry_col
    blk_row = lax.broadcasted_iota(jnp.int32, (n_blocks, blk), 0)

    score_scale = hd ** -0.5 * LOG2E

    for own in range(n_blocks):
        qn = _rms_norm_f32(q_ref[0, 0, own * blk:(own + 1) * blk, :].astype(jnp.float32),
                           qg_ref[...])
        q_s = (qn * score_scale).astype(jnp.bfloat16)

        s = lax.dot_general(kn_ref[own * blk:(own + 1) * blk, :], q_s, contract_last,
                            preferred_element_type=jnp.float32)
        s = jnp.where(causal, s, MASKED)
        m = jnp.max(s, axis=0, keepdims=True)
        p = jnp.exp2(s - m)
        l = jnp.sum(p, axis=0, keepdims=True)
        acc = jnp.dot(vt_ref[:, own * blk:(own + 1) * blk], p.astype(jnp.bfloat16),
                      preferred_element_type=jnp.float32)

        if own > 0:
            if own > MOBA_TOPK:
                g2 = lax.dot_general(km_hl, qn.astype(jnp.bfloat16), contract_last,
                                     preferred_element_type=jnp.float32)
                gate = g2[:n_blocks] + g2[n_blocks:]
                past = blk_row < own
                bias_rows = []
                for n in range(own):
                    g_n = gate[n:n + 1, :]
                    ahead = (gate > g_n) | ((gate == g_n) & (blk_row < n))
                    rank = jnp.sum(jnp.where(ahead & past, 1.0, 0.0), axis=0, keepdims=True)
                    bias_rows.append(jnp.where(rank < MOBA_TOPK, 0.0, MASKED))
            else:
                bias_rows = [None] * own

            for n in range(own):
                s = lax.dot_general(kn_ref[n * blk:(n + 1) * blk, :], q_s, contract_last,
                                    preferred_element_type=jnp.float32)
                if bias_rows[n] is not None:
                    s = s + bias_rows[n]
                m_new = jnp.maximum(m, jnp.max(s, axis=0, keepdims=True))
                alpha = jnp.exp2(m - m_new)
                p = jnp.exp2(s - m_new)
                l = alpha * l + jnp.sum(p, axis=0, keepdims=True)
                acc = alpha * acc + jnp.dot(vt_ref[:, n * blk:(n + 1) * blk],
                                            p.astype(jnp.bfloat16),
                                            preferred_element_type=jnp.float32)
                m = m_new

        out_t = acc / l
        o_ref[0, own * blk:(own + 1) * blk, :] = out_t.T.astype(o_ref.dtype)


def _moba(proj4d, q_gain, k_gain):
    _, bsz, seq, seg = proj4d.shape
    hd = seg // N_HEADS

    def head_spec(segment):
        return pl.BlockSpec((1, 1, seq, hd), lambda b, h: (segment, b, 0, h))

    gain_spec = pl.BlockSpec((1, hd), lambda b, h: (0, 0))
    return pl.pallas_call(
        _moba_kernel,
        grid=(bsz, N_HEADS),
        in_specs=[head_spec(SEG_Q), head_spec(SEG_K), head_spec(SEG_V), gain_spec, gain_spec],
        out_specs=pl.BlockSpec((1, seq, hd), lambda b, h: (b, 0, h)),
        out_shape=jax.ShapeDtypeStruct((bsz, seq, seg), jnp.bfloat16),
        scratch_shapes=[pltpu.VMEM((seq, hd), jnp.bfloat16),
                        pltpu.VMEM((hd, seq), jnp.bfloat16)],
        compiler_params=pltpu.CompilerParams(
            dimension_semantics=("arbitrary", "arbitrary"),
            vmem_limit_bytes=V7X_VMEM_LIMIT_BYTES,
        ),
        name="moba",
    )(proj4d, proj4d, proj4d, q_gain.reshape(1, hd), k_gain.reshape(1, hd))


def _outproj_kernel(tiles_per_seq, h_ref, b_ref, c_ref, zc_ref, za_ref, hh_ref, ch_ref,
                    a_ref, x_ref, cw_ref, cg_ref, ag_ref, w_ref, o_ref):
    f32 = jnp.float32
    tm = x_ref.shape[0]
    u = c_ref[0].astype(f32) * h_ref[0].astype(f32)
    halo = ch_ref[0].astype(f32) * hh_ref[0].astype(f32)
    first_in_seq = pl.program_id(0) % tiles_per_seq == 0
    halo = jnp.where(first_in_seq, 0.0, halo)
    u_ext = jnp.concatenate([halo, u], axis=0)
    u_m1 = pltpu.roll(u_ext, 1, 0)[SUBLANES:]
    u_m2 = pltpu.roll(u_ext, 2, 0)[SUBLANES:]
    cw = cw_ref[...]
    conv = u_m2 * cw[0:1] + u_m1 * cw[1:2]
    conv = conv + u * cw[2:3]
    y_conv = b_ref[0].astype(f32) * conv
    zc = zc_ref[0].astype(f32)
    y_conv = _rms_norm_f32(y_conv, cg_ref[...]) * (zc * jax.nn.sigmoid(zc))

    za = za_ref[0].astype(f32)
    y_attn = _rms_norm_f32(a_ref[...].astype(f32), ag_ref[...]) * (za * jax.nn.sigmoid(za))

    y = jnp.concatenate([y_conv, y_attn], axis=-1).astype(jnp.bfloat16)
    o_ref[...] = x_ref[...] + jnp.dot(y, w_ref[...], preferred_element_type=f32)


def _outproj(proj3d, attn2d, x2d, conv_w, conv_gain, attn_gain, w_out_bf16, seq):
    _, m, seg = proj3d.shape
    d = x2d.shape[1]
    tm = OUT_TILE_M
    tiles_per_seq = seq // tm

    def seg_spec(segment):
        return pl.BlockSpec((1, tm, seg), lambda i: (segment, i, 0))

    def halo_spec(segment):
        rows = tm // SUBLANES
        return pl.BlockSpec((1, SUBLANES, seg),
                            lambda i: (segment, jnp.maximum(i * rows - 1, 0), 0))

    def whole(shape):
        return pl.BlockSpec(shape, lambda i: (0,) * len(shape))

    return pl.pallas_call(
        functools.partial(_outproj_kernel, tiles_per_seq),
        grid=(m // tm,),
        in_specs=[
            seg_spec(SEG_H), seg_spec(SEG_B), seg_spec(SEG_C), seg_spec(SEG_ZC), seg_spec(SEG_ZA),
            halo_spec(SEG_H), halo_spec(SEG_C),
            pl.BlockSpec((tm, seg), lambda i: (i, 0)),
            pl.BlockSpec((tm, d), lambda i: (i, 0)),
            whole((CONV_WIDTH, seg)), whole((1, seg)), whole((1, seg)),
            whole(w_out_bf16.shape),
        ],
        out_specs=pl.BlockSpec((tm, d), lambda i: (i, 0)),
        out_shape=jax.ShapeDtypeStruct((m, d), jnp.float32),
        compiler_params=pltpu.CompilerParams(
            dimension_semantics=("arbitrary",),
            vmem_limit_bytes=V7X_VMEM_LIMIT_BYTES,
        ),
        name="outproj",
    )(proj3d, proj3d, proj3d, proj3d, proj3d, proj3d, proj3d, attn2d, x2d,
      conv_w, conv_gain.reshape(1, seg), attn_gain.reshape(1, seg), w_out_bf16)


def kernel(x, norm_gain, w_in, conv_w, q_norm_gain, k_norm_gain, conv_out_gain, attn_out_gain, w_out):
    bsz, seq, d = x.shape
    depth = norm_gain.shape[0]
    seg = w_in.shape[2] // N_SEG
    assert seq % MOBA_BLOCK == 0 and seq % OUT_TILE_M == 0 and (bsz * seq) % IN_TILE_M == 0
    assert w_out.shape[1] == 2 * seg and seg == N_HEADS * q_norm_gain.shape[1]
    x2d = x.reshape(bsz * seq, d)
    for layer in range(depth):
        proj = _inproj(x2d, norm_gain[layer], w_in[layer].astype(jnp.bfloat16))
        attn = _moba(proj.reshape(N_SEG, bsz, seq, seg), q_norm_gain[layer], k_norm_gain[layer])
        x2d = _outproj(proj, attn.reshape(bsz * seq, seg), x2d, conv_w[layer],
                       conv_out_gain[layer], attn_out_gain[layer],
                       w_out[layer].astype(jnp.bfloat16), seq)
    return x2d.reshape(bsz, seq, d)
```

```python
import functools
import math

import jax
import jax.numpy as jnp
from jax import lax
from jax.experimental import pallas as pl
from jax.experimental.pallas import tpu as pltpu

N_HEADS = 8
CONV_WIDTH = 3
MOBA_BLOCK = 256
MOBA_TOPK = 3
EPS = 1e-6
N_SEG = 8
SEG_H, SEG_B, SEG_C, SEG_ZC, SEG_Q, SEG_K, SEG_V, SEG_ZA = range(N_SEG)

MASKED = -1e30
LOG2E = math.log2(math.e)

V7X_VMEM_LIMIT_BYTES = 56 * 1024 * 1024
SUBLANES = 8

IN_TILE_M = 1024
OUT_TILE_M = 512


def _rms_norm_f32(x, gain):
    return x * lax.rsqrt(jnp.mean(x * x, axis=-1, keepdims=True) + EPS) * gain


def _inproj_kernel(x_ref, g_ref, w_ref, hg_ref, o_ref, h_ref):
    j = pl.program_id(1)

    @pl.when(j == 0)
    def _():
        h_ref[...] = _rms_norm_f32(x_ref[...], g_ref[...]).astype(h_ref.dtype)

    is_qk = (j == SEG_Q) | (j == SEG_K)

    @pl.when(is_qk)
    def _():
        acc = jnp.dot(h_ref[...], w_ref[...], preferred_element_type=jnp.float32)
        hd = acc.shape[1] // N_HEADS
        for head in range(N_HEADS):
            cols = slice(head * hd, (head + 1) * hd)
            o_ref[0, :, cols] = _rms_norm_f32(acc[:, cols], hg_ref[0, :, cols]).astype(o_ref.dtype)

    @pl.when(jnp.logical_not(is_qk))
    def _():
        acc = jnp.dot(h_ref[...], w_ref[...], preferred_element_type=jnp.float32)
        o_ref[0] = acc.astype(o_ref.dtype)


def _inproj(x2d, gain, w_bf16, head_gain):
    m, d = x2d.shape
    seg = w_bf16.shape[1] // N_SEG
    tm = IN_TILE_M
    return pl.pallas_call(
        _inproj_kernel,
        grid=(m // tm, N_SEG),
        in_specs=[
            pl.BlockSpec((tm, d), lambda i, j: (i, 0)),
            pl.BlockSpec((1, d), lambda i, j: (0, 0)),
            pl.BlockSpec((d, seg), lambda i, j: (0, j)),
            pl.BlockSpec((1, 1, seg), lambda i, j: (j, 0, 0)),
        ],
        out_specs=pl.BlockSpec((1, tm, seg), lambda i, j: (j, i, 0)),
        out_shape=jax.ShapeDtypeStruct((N_SEG, m, seg), jnp.bfloat16),
        scratch_shapes=[pltpu.VMEM((tm, d), jnp.bfloat16)],
        compiler_params=pltpu.CompilerParams(
            dimension_semantics=("arbitrary", "arbitrary"),
            vmem_limit_bytes=V7X_VMEM_LIMIT_BYTES,
        ),
        name="inproj",
    )(x2d, gain.reshape(1, d), w_bf16, head_gain)


def _moba_kernel(q_ref, k_ref, v_ref, o_ref, vt_ref):
    seq, hd = k_ref.shape[2], k_ref.shape[3]
    blk = MOBA_BLOCK
    n_blocks = seq // blk
    contract_last = (((1,), (1,)), ((), ()))

    k_mean = jnp.mean(k_ref[0, 0].astype(jnp.float32).reshape(n_blocks, blk, hd), axis=1)
    km_hi = k_mean.astype(jnp.bfloat16)
    km_lo = (k_mean - km_hi.astype(jnp.float32)).astype(jnp.bfloat16)
    km_hl = jnp.concatenate([km_hi, km_lo], axis=0)
    vt_ref[...] = v_ref[0, 0].astype(jnp.float32).T.astype(vt_ref.dtype)

    key_row = lax.broadcasted_iota(jnp.int32, (blk, blk), 0)
    qry_col = lax.broadcasted_iota(jnp.int32, (blk, blk), 1)
    causal = key_row <= qry_col
    blk_row = lax.broadcasted_iota(jnp.int32, (n_blocks, blk), 0)

    for own in range(n_blocks):
        q_s = q_ref[0, 0, own * blk:(own + 1) * blk, :]

        if own > MOBA_TOPK:
            g2 = lax.dot_general(km_hl, q_s, contract_last, preferred_element_type=jnp.float32)
            gate = g2[:n_blocks] + g2[n_blocks:]
            past = blk_row < own
            bias_rows = []
            for n in range(own):
                g_n = gate[n:n + 1, :]
                ahead = (gate > g_n) | ((gate == g_n) & (blk_row < n))
                rank = jnp.sum(jnp.where(ahead & past, 1.0, 0.0), axis=0, keepdims=True)
                bias_rows.append(jnp.where(rank < MOBA_TOPK, 0.0, MASKED))
        else:
            bias_rows = [None] * own

        n_keys = (own + 1) * blk
        parts = []
        for n in range(own + 1):
            part = lax.dot_general(k_ref[0, 0, n * blk:(n + 1) * blk, :], q_s, contract_last,
                                   preferred_element_type=jnp.float32)
            if n == own:
                part = jnp.where(causal, part, MASKED)
            elif bias_rows[n] is not None:
                part = part + bias_rows[n]
            parts.append(part)
        m = functools.reduce(jnp.maximum, [jnp.max(part, axis=0, keepdims=True) for part in parts])
        probs = [jnp.exp2(part - m) for part in parts]
        l = functools.reduce(jnp.add, [jnp.sum(p, axis=0, keepdims=True) for p in probs])
        p_all = jnp.concatenate([p.astype(jnp.bfloat16) for p in probs], axis=0)
        acc = jnp.dot(vt_ref[:, 0:n_keys], p_all, preferred_element_type=jnp.float32)
        out_t = acc / l
        o_ref[0, own * blk:(own + 1) * blk, :] = out_t.T.astype(o_ref.dtype)


def _moba(proj4d):
    _, bsz, seq, seg = proj4d.shape
    hd = seg // N_HEADS

    def head_spec(segment):
        return pl.BlockSpec((1, 1, seq, hd), lambda b, h: (segment, b, 0, h))

    return pl.pallas_call(
        _moba_kernel,
        grid=(bsz, N_HEADS),
        in_specs=[head_spec(SEG_Q), head_spec(SEG_K), head_spec(SEG_V)],
        out_specs=pl.BlockSpec((1, seq, hd), lambda b, h: (b, 0, h)),
        out_shape=jax.ShapeDtypeStruct((bsz, seq, seg), jnp.bfloat16),
        scratch_shapes=[pltpu.VMEM((hd, seq), jnp.bfloat16)],
        compiler_params=pltpu.CompilerParams(
            dimension_semantics=("arbitrary", "arbitrary"),
            vmem_limit_bytes=V7X_VMEM_LIMIT_BYTES,
        ),
        name="moba",
    )(proj4d, proj4d, proj4d)


def _outproj_kernel(tiles_per_seq, h_ref, b_ref, c_ref, zc_ref, za_ref, hh_ref, ch_ref,
                    a_ref, x_ref, cw_ref, cg_ref, ag_ref, w_ref, o_ref):
    f32 = jnp.float32
    u = c_ref[0].astype(f32) * h_ref[0].astype(f32)
    halo = ch_ref[0].astype(f32) * hh_ref[0].astype(f32)
    first_in_seq = pl.program_id(0) % tiles_per_seq == 0
    halo = jnp.where(first_in_seq, 0.0, halo)
    u_ext = jnp.concatenate([halo, u], axis=0)
    u_m1 = pltpu.roll(u_ext, 1, 0)[SUBLANES:]
    u_m2 = pltpu.roll(u_ext, 2, 0)[SUBLANES:]
    cw = cw_ref[...]
    conv = u_m2 * cw[0:1] + u_m1 * cw[1:2]
    conv = conv + u * cw[2:3]
    y_conv = b_ref[0].astype(f32) * conv
    zc = zc_ref[0].astype(f32)
    y_conv = _rms_norm_f32(y_conv, cg_ref[...]) * (zc * jax.nn.sigmoid(zc))

    za = za_ref[0].astype(f32)
    y_attn = _rms_norm_f32(a_ref[...].astype(f32), ag_ref[...]) * (za * jax.nn.sigmoid(za))

    y = jnp.concatenate([y_conv, y_attn], axis=-1).astype(jnp.bfloat16)
    o_ref[...] = x_ref[...] + jnp.dot(y, w_ref[...], preferred_element_type=f32)


def _outproj(proj3d, attn2d, x2d, conv_w, conv_gain, attn_gain, w_out_bf16, seq):
    _, m, seg = proj3d.shape
    d = x2d.shape[1]
    tm = OUT_TILE_M
    tiles_per_seq = seq // tm

    def seg_spec(segment):
        return pl.BlockSpec((1, tm, seg), lambda i: (segment, i, 0))

    def halo_spec(segment):
        rows = tm // SUBLANES
        return pl.BlockSpec((1, SUBLANES, seg),
                            lambda i: (segment, jnp.maximum(i * rows - 1, 0), 0))

    def whole(shape):
        return pl.BlockSpec(shape, lambda i: (0,) * len(shape))

    return pl.pallas_call(
        functools.partial(_outproj_kernel, tiles_per_seq),
        grid=(m // tm,),
        in_specs=[
            seg_spec(SEG_H), seg_spec(SEG_B), seg_spec(SEG_C), seg_spec(SEG_ZC), seg_spec(SEG_ZA),
            halo_spec(SEG_H), halo_spec(SEG_C),
            pl.BlockSpec((tm, seg), lambda i: (i, 0)),
            pl.BlockSpec((tm, d), lambda i: (i, 0)),
            whole((CONV_WIDTH, seg)), whole((1, seg)), whole((1, seg)),
            whole(w_out_bf16.shape),
        ],
        out_specs=pl.BlockSpec((tm, d), lambda i: (i, 0)),
        out_shape=jax.ShapeDtypeStruct((m, d), jnp.float32),
        compiler_params=pltpu.CompilerParams(
            dimension_semantics=("arbitrary",),
            vmem_limit_bytes=V7X_VMEM_LIMIT_BYTES,
        ),
        name="outproj",
    )(proj3d, proj3d, proj3d, proj3d, proj3d, proj3d, proj3d, attn2d, x2d,
      conv_w, conv_gain.reshape(1, seg), attn_gain.reshape(1, seg), w_out_bf16)


def _head_gains(q_gain, k_gain, seg):
    hd = q_gain.shape[0]
    score_scale = hd ** -0.5 * LOG2E
    rows = jnp.ones((N_SEG, seg), jnp.float32)
    rows = rows.at[SEG_Q].set(jnp.tile(q_gain * score_scale, N_HEADS))
    rows = rows.at[SEG_K].set(jnp.tile(k_gain, N_HEADS))
    return rows.reshape(N_SEG, 1, seg)


def kernel(x, norm_gain, w_in, conv_w, q_norm_gain, k_norm_gain, conv_out_gain, attn_out_gain, w_out):
    bsz, seq, d = x.shape
    depth = norm_gain.shape[0]
    seg = w_in.shape[2] // N_SEG
    assert seq % MOBA_BLOCK == 0 and seq % OUT_TILE_M == 0 and (bsz * seq) % IN_TILE_M == 0
    assert w_out.shape[1] == 2 * seg and seg == N_HEADS * q_norm_gain.shape[1]
    x2d = x.reshape(bsz * seq, d)
    for layer in range(depth):
        proj = _inproj(x2d, norm_gain[layer], w_in[layer].astype(jnp.bfloat16),
                       _head_gains(q_norm_gain[layer], k_norm_gain[layer], seg))
        attn = _moba(proj.reshape(N_SEG, bsz, seq, seg))
        x2d = _outproj(proj, attn.reshape(bsz * seq, seg), x2d, conv_w[layer],
                       conv_out_gain[layer], attn_out_gain[layer],
                       w_out[layer].astype(jnp.bfloat16), seq)
    return x2d.reshape(bsz, seq, d)
```

```python
import functools
import math

import jax
import jax.numpy as jnp
from jax import lax
from jax.experimental import pallas as pl
from jax.experimental.pallas import tpu as pltpu

N_HEADS = 8
CONV_WIDTH = 3
MOBA_BLOCK = 256
MOBA_TOPK = 3
EPS = 1e-6
N_SEG = 8
SEG_H, SEG_B, SEG_C, SEG_ZC, SEG_Q, SEG_K, SEG_V, SEG_ZA = range(N_SEG)

MASKED = -1e30
LOG2E = math.log2(math.e)
MAX_SHIFT_GAP = 100.0
BF16_NORM_SLACK = 1.0 + 2.0 ** -6

V7X_VMEM_LIMIT_BYTES = 56 * 1024 * 1024
SUBLANES = 8
LANES = 128

IN_TILE_M = 1024
OUT_TILE_M = 512


def _rms_norm_f32(x, gain):
    return x * lax.rsqrt(jnp.mean(x * x, axis=-1, keepdims=True) + EPS) * gain


def _inproj_kernel(x_ref, g_ref, w_ref, hg_ref, o_ref, h_ref):
    j = pl.program_id(1)

    @pl.when(j == 0)
    def _():
        h_ref[...] = _rms_norm_f32(x_ref[...], g_ref[...]).astype(h_ref.dtype)

    is_qk = (j == SEG_Q) | (j == SEG_K)

    @pl.when(is_qk)
    def _():
        acc = jnp.dot(h_ref[...], w_ref[...], preferred_element_type=jnp.float32)
        hd = acc.shape[1] // N_HEADS
        for head in range(N_HEADS):
            cols = slice(head * hd, (head + 1) * hd)
            o_ref[0, :, cols] = _rms_norm_f32(acc[:, cols], hg_ref[0, :, cols]).astype(o_ref.dtype)

    @pl.when(jnp.logical_not(is_qk))
    def _():
        acc = jnp.dot(h_ref[...], w_ref[...], preferred_element_type=jnp.float32)
        o_ref[0] = acc.astype(o_ref.dtype)


def _inproj(x2d, gain, w_bf16, head_gain):
    m, d = x2d.shape
    seg = w_bf16.shape[1] // N_SEG
    tm = IN_TILE_M
    return pl.pallas_call(
        _inproj_kernel,
        grid=(m // tm, N_SEG),
        in_specs=[
            pl.BlockSpec((tm, d), lambda i, j: (i, 0)),
            pl.BlockSpec((1, d), lambda i, j: (0, 0)),
            pl.BlockSpec((d, seg), lambda i, j: (0, j)),
            pl.BlockSpec((1, 1, seg), lambda i, j: (j, 0, 0)),
        ],
        out_specs=pl.BlockSpec((1, tm, seg), lambda i, j: (j, i, 0)),
        out_shape=jax.ShapeDtypeStruct((N_SEG, m, seg), jnp.bfloat16),
        scratch_shapes=[pltpu.VMEM((tm, d), jnp.bfloat16)],
        compiler_params=pltpu.CompilerParams(
            dimension_semantics=("arbitrary", "arbitrary"),
            vmem_limit_bytes=V7X_VMEM_LIMIT_BYTES,
        ),
        name="inproj",
    )(x2d, gain.reshape(1, d), w_bf16, head_gain)


def _moba_kernel(par_ref, q_ref, k_ref, v_ref, o_ref, vt_ref, qt_ref):
    f32, bf16 = jnp.float32, jnp.bfloat16
    seq, hd = k_ref.shape[2], k_ref.shape[3]
    blk = MOBA_BLOCK
    n_blocks = seq // blk
    assert n_blocks <= SUBLANES and hd == LANES

    k_mean = jnp.mean(k_ref[0, 0].astype(f32).reshape(n_blocks, blk, hd), axis=1)
    km_hi = k_mean.astype(bf16)
    km_lo = (k_mean - km_hi.astype(f32)).astype(bf16)
    km_hl = jnp.concatenate([km_hi, km_lo], axis=0)
    vt_ref[...] = v_ref[0, 0].astype(f32).T.astype(bf16)
    qt_ref[...] = q_ref[0, 0].astype(f32).T.astype(bf16)

    key_row = lax.broadcasted_iota(jnp.int32, (blk, blk), 0)
    qry_col = lax.broadcasted_iota(jnp.int32, (blk, blk), 1)
    causal = key_row <= qry_col
    blk_row = lax.broadcasted_iota(jnp.int32, (n_blocks, blk), 0)

    def block_bias(own, q_t):
        if own <= MOBA_TOPK:
            return None
        g2 = jnp.dot(km_hl, q_t, preferred_element_type=f32)
        gate = g2[:n_blocks] + g2[n_blocks:]
        rank = jnp.zeros(gate.shape, f32)
        for m in range(own):
            g_m = gate[m:m + 1, :]
            ahead = (g_m > gate) | ((g_m == gate) & (blk_row > m))
            rank = rank + jnp.where(ahead, 1.0, 0.0)
        return jnp.where((rank < MOBA_TOPK) | (blk_row >= own), 0.0, MASKED)

    def finish(own, probs, l):
        n_keys = (own + 1) * blk
        p_all = jnp.concatenate([p.astype(bf16) for p in probs], axis=0)
        acc = jnp.dot(vt_ref[:, 0:n_keys], p_all, preferred_element_type=f32)
        out_t = acc / l
        o_ref[0, own * blk:(own + 1) * blk, :] = out_t.T.astype(o_ref.dtype)

    def tile_exact(own):
        q_t = qt_ref[:, own * blk:(own + 1) * blk]
        bias = block_bias(own, q_t)
        parts = []
        for n in range(own + 1):
            part = jnp.dot(k_ref[0, 0, n * blk:(n + 1) * blk, :], q_t, preferred_element_type=f32)
            if n == own:
                part = jnp.where(causal, part, MASKED)
            elif bias is not None:
                part = part + bias[n:n + 1, :]
            parts.append(part)
        m = functools.reduce(jnp.maximum, [jnp.max(part, axis=0, keepdims=True) for part in parts])
        probs = [jnp.exp2(part - m) for part in parts]
        l = functools.reduce(jnp.add, [jnp.sum(p, axis=0, keepdims=True) for p in probs])
        finish(own, probs, l)

    def tile_bounded(own):
        q_t = qt_ref[:, own * blk:(own + 1) * blk]
        q_f = q_t.astype(f32)
        bound = jnp.sqrt(jnp.sum(q_f * q_f, axis=0, keepdims=True)) * par_ref[0]
        bias = block_bias(own, q_t)
        if bias is None:
            bias = jnp.zeros((n_blocks, blk), f32)
        shift = jnp.broadcast_to(-bound, (SUBLANES, blk))
        pad = jnp.zeros((hd - n_blocks - SUBLANES, blk), f32)
        extra = jnp.concatenate([bias, shift, pad], axis=0).astype(bf16)
        rhs = jnp.concatenate([q_t, extra], axis=0)
        lane = lax.broadcasted_iota(jnp.int32, (blk, hd), 1)
        probs = []
        for n in range(own + 1):
            pick = jnp.where((lane == n) | (lane == n_blocks), 1.0, 0.0).astype(bf16)
            lhs = jnp.concatenate([k_ref[0, 0, n * blk:(n + 1) * blk, :], pick], axis=1)
            part = jnp.dot(lhs, rhs, preferred_element_type=f32)
            if n == own:
                part = jnp.where(causal, part, MASKED)
            probs.append(jnp.exp2(part))
        l = functools.reduce(jnp.add, [jnp.sum(p, axis=0, keepdims=True) for p in probs])
        finish(own, probs, l)

    bounded_ok = par_ref[1] > 0.5

    @pl.when(bounded_ok)
    def _():
        for own in range(n_blocks):
            tile_bounded(own)

    @pl.when(jnp.logical_not(bounded_ok))
    def _():
        for own in range(n_blocks):
            tile_exact(own)


def _moba(proj4d, params):
    _, bsz, seq, seg = proj4d.shape
    hd = seg // N_HEADS

    def head_spec(segment):
        return pl.BlockSpec((1, 1, seq, hd), lambda b, h: (segment, b, 0, h))

    return pl.pallas_call(
        _moba_kernel,
        grid=(bsz, N_HEADS),
        in_specs=[pl.BlockSpec(memory_space=pltpu.SMEM),
                  head_spec(SEG_Q), head_spec(SEG_K), head_spec(SEG_V)],
        out_specs=pl.BlockSpec((1, seq, hd), lambda b, h: (b, 0, h)),
        out_shape=jax.ShapeDtypeStruct((bsz, seq, seg), jnp.bfloat16),
        scratch_shapes=[pltpu.VMEM((hd, seq), jnp.bfloat16), pltpu.VMEM((hd, seq), jnp.bfloat16)],
        compiler_params=pltpu.CompilerParams(
            dimension_semantics=("arbitrary", "arbitrary"),
            vmem_limit_bytes=V7X_VMEM_LIMIT_BYTES,
        ),
        name="moba",
    )(params, proj4d, proj4d, proj4d)


def _outproj_kernel(tiles_per_seq, h_ref, b_ref, c_ref, zc_ref, za_ref, hh_ref, ch_ref,
                    a_ref, x_ref, cw_ref, cg_ref, ag_ref, w_ref, o_ref):
    f32 = jnp.float32
    u = c_ref[0].astype(f32) * h_ref[0].astype(f32)
    halo = ch_ref[0].astype(f32) * hh_ref[0].astype(f32)
    first_in_seq = pl.program_id(0) % tiles_per_seq == 0
    halo = jnp.where(first_in_seq, 0.0, halo)
    u_ext = jnp.concatenate([halo, u], axis=0)
    u_m1 = pltpu.roll(u_ext, 1, 0)[SUBLANES:]
    u_m2 = pltpu.roll(u_ext, 2, 0)[SUBLANES:]
    cw = cw_ref[...]
    conv = u_m2 * cw[0:1] + u_m1 * cw[1:2]
    conv = conv + u * cw[2:3]
    y_conv = b_ref[0].astype(f32) * conv
    zc = zc_ref[0].astype(f32)
    y_conv = _rms_norm_f32(y_conv, cg_ref[...]) * (zc * jax.nn.sigmoid(zc))

    za = za_ref[0].astype(f32)
    y_attn = _rms_norm_f32(a_ref[...].astype(f32), ag_ref[...]) * (za * jax.nn.sigmoid(za))

    y = jnp.concatenate([y_conv, y_attn], axis=-1).astype(jnp.bfloat16)
    o_ref[...] = x_ref[...] + jnp.dot(y, w_ref[...], preferred_element_type=f32)


def _outproj(proj3d, attn2d, x2d, conv_w, conv_gain, attn_gain, w_out_bf16, seq):
    _, m, seg = proj3d.shape
    d = x2d.shape[1]
    tm = OUT_TILE_M
    tiles_per_seq = seq // tm

    def seg_spec(segment):
        return pl.BlockSpec((1, tm, seg), lambda i: (segment, i, 0))

    def halo_spec(segment):
        rows = tm // SUBLANES
        return pl.BlockSpec((1, SUBLANES, seg),
                            lambda i: (segment, jnp.maximum(i * rows - 1, 0), 0))

    def whole(shape):
        return pl.BlockSpec(shape, lambda i: (0,) * len(shape))

    return pl.pallas_call(
        functools.partial(_outproj_kernel, tiles_per_seq),
        grid=(m // tm,),
        in_specs=[
            seg_spec(SEG_H), seg_spec(SEG_B), seg_spec(SEG_C), seg_spec(SEG_ZC), seg_spec(SEG_ZA),
            halo_spec(SEG_H), halo_spec(SEG_C),
            pl.BlockSpec((tm, seg), lambda i: (i, 0)),
            pl.BlockSpec((tm, d), lambda i: (i, 0)),
            whole((CONV_WIDTH, seg)), whole((1, seg)), whole((1, seg)),
            whole(w_out_bf16.shape),
        ],
        out_specs=pl.BlockSpec((tm, d), lambda i: (i, 0)),
        out_shape=jax.ShapeDtypeStruct((m, d), jnp.float32),
        compiler_params=pltpu.CompilerParams(
            dimension_semantics=("arbitrary",),
            vmem_limit_bytes=V7X_VMEM_LIMIT_BYTES,
        ),
        name="outproj",
    )(proj3d, proj3d, proj3d, proj3d, proj3d, proj3d, proj3d, attn2d, x2d,
      conv_w, conv_gain.reshape(1, seg), attn_gain.reshape(1, seg), w_out_bf16)


def _head_gains(q_gain, k_gain, seg):
    hd = q_gain.shape[0]
    score_scale = hd ** -0.5 * LOG2E
    rows = jnp.ones((N_SEG, seg), jnp.float32)
    rows = rows.at[SEG_Q].set(jnp.tile(q_gain * score_scale, N_HEADS))
    rows = rows.at[SEG_K].set(jnp.tile(k_gain, N_HEADS))
    return rows.reshape(N_SEG, 1, seg)


def _softmax_shift_params(q_gain, k_gain):
    hd = q_gain.shape[0]
    k_max = math.sqrt(hd) * jnp.max(jnp.abs(k_gain)) * BF16_NORM_SLACK
    q_max = math.sqrt(hd) * jnp.max(jnp.abs(q_gain)) * (hd ** -0.5 * LOG2E) * BF16_NORM_SLACK
    bounded_ok = 2.0 * q_max * k_max <= MAX_SHIFT_GAP
    return jnp.stack([k_max, bounded_ok.astype(jnp.float32)]).astype(jnp.float32)


def kernel(x, norm_gain, w_in, conv_w, q_norm_gain, k_norm_gain, conv_out_gain, attn_out_gain, w_out):
    bsz, seq, d = x.shape
    depth = norm_gain.shape[0]
    seg = w_in.shape[2] // N_SEG
    assert seq % MOBA_BLOCK == 0 and seq % OUT_TILE_M == 0 and (bsz * seq) % IN_TILE_M == 0
    assert w_out.shape[1] == 2 * seg and seg == N_HEADS * q_norm_gain.shape[1]
    x2d = x.reshape(bsz * seq, d)
    for layer in range(depth):
        proj = _inproj(x2d, norm_gain[layer], w_in[layer].astype(jnp.bfloat16),
                       _head_gains(q_norm_gain[layer], k_norm_gain[layer], seg))
        attn = _moba(proj.reshape(N_SEG, bsz, seq, seg),
                     _softmax_shift_params(q_norm_gain[layer], k_norm_gain[layer]))
        x2d = _outproj(proj, attn.reshape(bsz * seq, seg), x2d, conv_w[layer],
                       conv_out_gain[layer], attn_out_gain[layer],
                       w_out[layer].astype(jnp.bfloat16), seq)
    return x2d.reshape(bsz, seq, d)
```

```python
import functools
import math

import jax
import jax.numpy as jnp
from jax import lax
from jax.experimental import pallas as pl
from jax.experimental.pallas import tpu as pltpu

N_HEADS = 8
CONV_WIDTH = 3
MOBA_BLOCK = 256
MOBA_TOPK = 3
EPS = 1e-6
N_SEG = 8
SEG_H, SEG_B, SEG_C, SEG_ZC, SEG_Q, SEG_K, SEG_V, SEG_ZA = range(N_SEG)

MASKED = -1e30
LOG2E = math.log2(math.e)
MAX_SHIFT_GAP = 100.0
BF16_NORM_SLACK = 1.0 + 2.0 ** -6

V7X_VMEM_LIMIT_BYTES = 56 * 1024 * 1024
SUBLANES = 8
LANES = 128

IN_TILE_M = 1024
OUT_TILE_M = 512
OUT_SLAB_M = 256


def _rms_norm_f32(x, gain):
    return x * lax.rsqrt(jnp.mean(x * x, axis=-1, keepdims=True) + EPS) * gain


def _inproj_kernel(x_ref, g_ref, w_ref, hg_ref, wo_ref, o_ref, qt_ref, vt_ref, wob_ref, h_ref):
    j = pl.program_id(1)
    f32 = jnp.float32
    hd = qt_ref.shape[1]

    wob_ref[...] = wo_ref[...].astype(wob_ref.dtype)

    @pl.when(j == 0)
    def _():
        h_ref[...] = _rms_norm_f32(x_ref[...], g_ref[...]).astype(h_ref.dtype)

    def project():
        return jnp.dot(h_ref[...], w_ref[...], preferred_element_type=f32)

    def head_cols(head):
        return slice(head * hd, (head + 1) * hd)

    @pl.when(j == SEG_Q)
    def _():
        acc = project()
        for head in range(N_HEADS):
            cols = head_cols(head)
            qt_ref[head] = _rms_norm_f32(acc[:, cols], hg_ref[0, :, cols]).T.astype(qt_ref.dtype)

    @pl.when(j == SEG_K)
    def _():
        acc = project()
        for head in range(N_HEADS):
            cols = head_cols(head)
            o_ref[0, :, cols] = _rms_norm_f32(acc[:, cols], hg_ref[0, :, cols]).astype(o_ref.dtype)

    @pl.when(j == SEG_V)
    def _():
        acc = project()
        for head in range(N_HEADS):
            vt_ref[head] = acc[:, head_cols(head)].T.astype(vt_ref.dtype)

    @pl.when((j != SEG_Q) & (j != SEG_K) & (j != SEG_V))
    def _():
        o_ref[0] = project().astype(o_ref.dtype)


def _inproj(x2d, gain, w_bf16, head_gain, w_out):
    m, d = x2d.shape
    seg = w_bf16.shape[1] // N_SEG
    hd = seg // N_HEADS
    tm = IN_TILE_M
    n_tiles = m // tm
    wo_rows = w_out.shape[0] // (n_tiles * N_SEG)
    assert wo_rows * n_tiles * N_SEG == w_out.shape[0] and wo_rows % (2 * SUBLANES) == 0

    def slab(i, j):
        jj = jnp.where(j == SEG_Q, SEG_Q - 1, jnp.where(j == SEG_V, SEG_V - 1, j))
        return (jj, i, 0)

    def wo_block(i, j):
        return (i * N_SEG + j, 0)

    heads_t = pl.BlockSpec((N_HEADS, hd, tm), lambda i, j: (0, 0, i))
    return pl.pallas_call(
        _inproj_kernel,
        grid=(n_tiles, N_SEG),
        in_specs=[
            pl.BlockSpec((tm, d), lambda i, j: (i, 0)),
            pl.BlockSpec((1, d), lambda i, j: (0, 0)),
            pl.BlockSpec((d, seg), lambda i, j: (0, j)),
            pl.BlockSpec((1, 1, seg), lambda i, j: (j, 0, 0)),
            pl.BlockSpec((wo_rows, w_out.shape[1]), wo_block),
        ],
        out_specs=[
            pl.BlockSpec((1, tm, seg), slab),
            heads_t, heads_t,
            pl.BlockSpec((wo_rows, w_out.shape[1]), wo_block),
        ],
        out_shape=[
            jax.ShapeDtypeStruct((N_SEG, m, seg), jnp.bfloat16),
            jax.ShapeDtypeStruct((N_HEADS, hd, m), jnp.bfloat16),
            jax.ShapeDtypeStruct((N_HEADS, hd, m), jnp.bfloat16),
            jax.ShapeDtypeStruct(w_out.shape, jnp.bfloat16),
        ],
        scratch_shapes=[pltpu.VMEM((tm, d), jnp.bfloat16)],
        compiler_params=pltpu.CompilerParams(
            dimension_semantics=("arbitrary", "arbitrary"),
            vmem_limit_bytes=V7X_VMEM_LIMIT_BYTES,
        ),
        name="inproj",
    )(x2d, gain.reshape(1, d), w_bf16, head_gain, w_out)


def _moba_kernel(par_ref, qt_ref, k_ref, vt_ref, o_ref):
    f32, bf16 = jnp.float32, jnp.bfloat16
    seq, hd = k_ref.shape[2], k_ref.shape[3]
    blk = MOBA_BLOCK
    n_blocks = seq // blk
    assert n_blocks == SUBLANES and hd == LANES

    k_mean = jnp.mean(k_ref[0, 0].astype(f32).reshape(n_blocks, blk, hd), axis=1)
    km_hi = k_mean.astype(bf16)
    km_lo = (k_mean - km_hi.astype(f32)).astype(bf16)
    km_hl = jnp.concatenate([km_hi, km_lo], axis=0)

    key_row = lax.broadcasted_iota(jnp.int32, (blk, blk), 0)
    qry_col = lax.broadcasted_iota(jnp.int32, (blk, blk), 1)
    causal = key_row <= qry_col
    blk_row = lax.broadcasted_iota(jnp.int32, (n_blocks, blk), 0)

    def block_bias(own, q_t):
        if own <= MOBA_TOPK:
            return None
        g2 = jnp.dot(km_hl, q_t, preferred_element_type=f32)
        gate = g2[:n_blocks] + g2[n_blocks:]
        rank = jnp.zeros(gate.shape, f32)
        for m in range(own):
            g_m = gate[m:m + 1, :]
            ahead = (g_m > gate) | ((g_m == gate) & (blk_row > m))
            rank = rank + jnp.where(ahead, 1.0, 0.0)
        return jnp.where((rank < MOBA_TOPK) | (blk_row >= own), 0.0, MASKED)

    def finish(own, probs, l):
        n_keys = (own + 1) * blk
        p_all = jnp.concatenate([p.astype(bf16) for p in probs], axis=0)
        acc = jnp.dot(vt_ref[0, :, 0:n_keys], p_all, preferred_element_type=f32)
        out_t = acc / l
        o_ref[0, own * blk:(own + 1) * blk, :] = out_t.T.astype(o_ref.dtype)

    def tile_exact(own):
        q_t = qt_ref[0, :, own * blk:(own + 1) * blk]
        bias = block_bias(own, q_t)
        parts = []
        for n in range(own + 1):
            part = jnp.dot(k_ref[0, 0, n * blk:(n + 1) * blk, :], q_t, preferred_element_type=f32)
            if n == own:
                part = jnp.where(causal, part, MASKED)
            elif bias is not None:
                part = part + bias[n:n + 1, :]
            parts.append(part)
        m = functools.reduce(jnp.maximum, [jnp.max(part, axis=0, keepdims=True) for part in parts])
        probs = [jnp.exp2(part - m) for part in parts]
        l = functools.reduce(jnp.add, [jnp.sum(p, axis=0, keepdims=True) for p in probs])
        finish(own, probs, l)

    def tile_bounded(own):
        q_t = qt_ref[0, :, own * blk:(own + 1) * blk]
        q_f = q_t.astype(f32)
        bound = jnp.sqrt(jnp.sum(q_f * q_f, axis=0, keepdims=True)) * par_ref[0]
        bias = block_bias(own, q_t)
        if bias is None:
            bias = jnp.zeros((n_blocks, blk), f32)
        shift = jnp.broadcast_to(-bound, (SUBLANES, blk))
        pad = jnp.zeros((hd - n_blocks - SUBLANES, blk), f32)
        extra = jnp.concatenate([bias, shift, pad], axis=0).astype(bf16)
        rhs = jnp.concatenate([q_t, extra], axis=0)
        lane = lax.broadcasted_iota(jnp.int32, (blk, hd), 1)
        probs = []
        for n in range(own + 1):
            pick = jnp.where((lane == n) | (lane == n_blocks), 1.0, 0.0).astype(bf16)
            lhs = jnp.concatenate([k_ref[0, 0, n * blk:(n + 1) * blk, :], pick], axis=1)
            part = jnp.dot(lhs, rhs, preferred_element_type=f32)
            if n == own:
                part = jnp.where(causal, part, MASKED)
            probs.append(jnp.exp2(part))
        l = functools.reduce(jnp.add, [jnp.sum(p, axis=0, keepdims=True) for p in probs])
        finish(own, probs, l)

    bounded_ok = par_ref[1] > 0.5
    tile_order = range(n_blocks)

    @pl.when(bounded_ok)
    def _():
        for own in tile_order:
            tile_bounded(own)

    @pl.when(jnp.logical_not(bounded_ok))
    def _():
        for own in tile_order:
            tile_exact(own)


def _moba(proj4d, q_t, v_t, params):
    _, bsz, seq, seg = proj4d.shape
    hd = seg // N_HEADS
    heads_t = pl.BlockSpec((1, hd, seq), lambda b, h: (h, 0, b))
    return pl.pallas_call(
        _moba_kernel,
        grid=(bsz, N_HEADS),
        in_specs=[pl.BlockSpec(memory_space=pltpu.SMEM), heads_t,
                  pl.BlockSpec((1, 1, seq, hd), lambda b, h: (SEG_K, b, 0, h)), heads_t],
        out_specs=pl.BlockSpec((1, seq, hd), lambda b, h: (b, 0, h)),
        out_shape=jax.ShapeDtypeStruct((bsz, seq, seg), jnp.bfloat16),
        compiler_params=pltpu.CompilerParams(
            dimension_semantics=("arbitrary", "arbitrary"),
            vmem_limit_bytes=V7X_VMEM_LIMIT_BYTES,
        ),
        name="moba",
    )(params, q_t, proj4d, v_t)


def _outproj_kernel(tiles_per_seq, h_ref, b_ref, c_ref, zc_ref, za_ref, hh_ref, ch_ref,
                    a_ref, x_ref, cw_ref, cg_ref, ag_ref, w_ref, o_ref):
    f32 = jnp.float32
    tm = o_ref.shape[0]
    lead = 2 * SUBLANES
    first_in_seq = pl.program_id(0) % tiles_per_seq == 0
    cw = cw_ref[...]
    for r0 in range(0, tm, OUT_SLAB_M):
        rows = slice(r0, r0 + OUT_SLAB_M)
        if r0 == 0:
            halo = ch_ref[0].astype(f32) * hh_ref[0].astype(f32)
            halo = jnp.where(first_in_seq, 0.0, halo)
            u = c_ref[0, rows].astype(f32) * h_ref[0, rows].astype(f32)
            u_ext = jnp.concatenate([jnp.zeros_like(halo), halo, u], axis=0)
        else:
            ext = slice(r0 - lead, r0 + OUT_SLAB_M)
            u_ext = c_ref[0, ext].astype(f32) * h_ref[0, ext].astype(f32)
        u = u_ext[lead:]
        u_m1 = pltpu.roll(u_ext, 1, 0)[lead:]
        u_m2 = pltpu.roll(u_ext, 2, 0)[lead:]
        conv = u_m2 * cw[0:1] + u_m1 * cw[1:2]
        conv = conv + u * cw[2:3]
        y_conv = b_ref[0, rows].astype(f32) * conv
        zc = zc_ref[0, rows].astype(f32)
        za = za_ref[0, rows].astype(f32)
        attn = a_ref[rows].astype(f32)
        rs_conv = lax.rsqrt(jnp.mean(y_conv * y_conv, axis=-1, keepdims=True) + EPS)
        rs_attn = lax.rsqrt(jnp.mean(attn * attn, axis=-1, keepdims=True) + EPS)
        g_conv = (y_conv * cg_ref[...] * (zc * jax.nn.sigmoid(zc))).astype(jnp.bfloat16)
        g_attn = (attn * ag_ref[...] * (za * jax.nn.sigmoid(za))).astype(jnp.bfloat16)
        d_conv = g_conv.shape[1]
        p_conv = jnp.dot(g_conv, w_ref[:d_conv], preferred_element_type=f32)
        p_attn = jnp.dot(g_attn, w_ref[d_conv:], preferred_element_type=f32)
        o_ref[rows] = x_ref[rows] + (rs_conv * p_conv + rs_attn * p_attn)


def _outproj(proj3d, attn2d, x2d, conv_w, conv_gain, attn_gain, w_out_bf16, seq):
    _, m, seg = proj3d.shape
    d = x2d.shape[1]
    tm = OUT_TILE_M
    tiles_per_seq = seq // tm

    def seg_spec(segment):
        return pl.BlockSpec((1, tm, seg), lambda i: (segment, i, 0))

    def halo_spec(segment):
        rows = tm // SUBLANES
        return pl.BlockSpec((1, SUBLANES, seg),
                            lambda i: (segment, jnp.maximum(i * rows - 1, 0), 0))

    def whole(shape):
        return pl.BlockSpec(shape, lambda i: (0,) * len(shape))

    return pl.pallas_call(
        functools.partial(_outproj_kernel, tiles_per_seq),
        grid=(m // tm,),
        in_specs=[
            seg_spec(SEG_H), seg_spec(SEG_B), seg_spec(SEG_C), seg_spec(SEG_ZC), seg_spec(SEG_ZA),
            halo_spec(SEG_H), halo_spec(SEG_C),
            pl.BlockSpec((tm, seg), lambda i: (i, 0)),
            pl.BlockSpec((tm, d), lambda i: (i, 0)),
            whole((CONV_WIDTH, seg)), whole((1, seg)), whole((1, seg)),
            whole(w_out_bf16.shape),
        ],
        out_specs=pl.BlockSpec((tm, d), lambda i: (i, 0)),
        out_shape=jax.ShapeDtypeStruct((m, d), jnp.float32),
        compiler_params=pltpu.CompilerParams(
            dimension_semantics=("arbitrary",),
            vmem_limit_bytes=V7X_VMEM_LIMIT_BYTES,
        ),
        name="outproj",
    )(proj3d, proj3d, proj3d, proj3d, proj3d, proj3d, proj3d, attn2d, x2d,
      conv_w, conv_gain.reshape(1, seg), attn_gain.reshape(1, seg), w_out_bf16)


def _head_gains(q_gain, k_gain, seg):
    hd = q_gain.shape[0]
    score_scale = hd ** -0.5 * LOG2E
    rows = jnp.ones((N_SEG, seg), jnp.float32)
    rows = rows.at[SEG_Q].set(jnp.tile(q_gain * score_scale, N_HEADS))
    rows = rows.at[SEG_K].set(jnp.tile(k_gain, N_HEADS))
    return rows.reshape(N_SEG, 1, seg)


def _softmax_shift_params(q_gain, k_gain):
    hd = q_gain.shape[0]
    k_max = math.sqrt(hd) * jnp.max(jnp.abs(k_gain)) * BF16_NORM_SLACK
    q_max = math.sqrt(hd) * jnp.max(jnp.abs(q_gain)) * (hd ** -0.5 * LOG2E) * BF16_NORM_SLACK
    bounded_ok = 2.0 * q_max * k_max <= MAX_SHIFT_GAP
    return jnp.stack([k_max, bounded_ok.astype(jnp.float32)]).astype(jnp.float32)


def kernel(x, norm_gain, w_in, conv_w, q_norm_gain, k_norm_gain, conv_out_gain, attn_out_gain, w_out):
    bsz, seq, d = x.shape
    depth = norm_gain.shape[0]
    seg = w_in.shape[2] // N_SEG
    assert seq % MOBA_BLOCK == 0 and seq % OUT_TILE_M == 0 and (bsz * seq) % IN_TILE_M == 0
    assert w_out.shape[1] == 2 * seg and seg == N_HEADS * q_norm_gain.shape[1]
    x2d = x.reshape(bsz * seq, d)
    for layer in range(depth):
        proj, q_t, v_t, w_out_bf16 = _inproj(
            x2d, norm_gain[layer], w_in[layer].astype(jnp.bfloat16),
            _head_gains(q_norm_gain[layer], k_norm_gain[layer], seg), w_out[layer])
        attn = _moba(proj.reshape(N_SEG, bsz, seq, seg), q_t, v_t,
                     _softmax_shift_params(q_norm_gain[layer], k_norm_gain[layer]))
        x2d = _outproj(proj, attn.reshape(bsz * seq, seg), x2d, conv_w[layer],
                       conv_out_gain[layer], attn_out_gain[layer], w_out_bf16, seq)
    return x2d.reshape(bsz, seq, d)
```

```python
import functools
import math

import jax
import jax.numpy as jnp
from jax import lax
from jax.experimental import pallas as pl
from jax.experimental.pallas import tpu as pltpu

N_HEADS = 8
CONV_WIDTH = 3
MOBA_BLOCK = 256
MOBA_TOPK = 3
EPS = 1e-6
N_SEG = 8
SEG_H, SEG_B, SEG_C, SEG_ZC, SEG_Q, SEG_K, SEG_V, SEG_ZA = range(N_SEG)
N_SLABS = 3
SLAB_CONV, SLAB_K, SLAB_GATE = range(N_SLABS)

MASKED = -1e30
LOG2E = math.log2(math.e)
MAX_SHIFT_GAP = 100.0
BF16_NORM_SLACK = 1.0 + 2.0 ** -6

V7X_VMEM_LIMIT_BYTES = 56 * 1024 * 1024
SUBLANES = 8
LANES = 128

IN_TILE_M = 1024
OUT_TILE_M = 512
OUT_SLAB_M = 256
MOBA_HEADS_PER_STEP = 1


def _rms_norm_f32(x, gain):
    return x * lax.rsqrt(jnp.mean(x * x, axis=-1, keepdims=True) + EPS) * gain


def _inproj_kernel(tiles_per_seq, x_ref, g_ref, w_ref, hg_ref, wo_ref, cw_ref, cg_ref,
                   o_ref, qt_ref, vt_ref, wob_ref, rs_ref,
                   h_ref, hc_ref, bb_ref, yp_ref, ss_ref, halo_ref):
    i, j = pl.program_id(0), pl.program_id(1)
    f32 = jnp.float32
    hd = qt_ref.shape[1]

    wob_ref[...] = wo_ref[...].astype(wob_ref.dtype)

    def project(rows=slice(None)):
        return jnp.dot(h_ref[rows], w_ref[...], preferred_element_type=f32)

    half = h_ref.shape[0] // 2
    row_halves = (slice(0, half), slice(half, 2 * half))

    def head_cols(head):
        return slice(head * hd, (head + 1) * hd)

    def silu(z):
        return z * jax.nn.sigmoid(z)

    @pl.when(j == SEG_H)
    def _():
        for rows in row_halves:
            h = _rms_norm_f32(x_ref[rows], g_ref[...]).astype(h_ref.dtype)
            h_ref[rows] = h
            hc_ref[rows] = jnp.dot(h, w_ref[...], preferred_element_type=f32).astype(hc_ref.dtype)

    @pl.when(j == SEG_B)
    def _():
        bb_ref[...] = project().astype(bb_ref.dtype)

    @pl.when(j == SEG_C)
    def _():
        cw = cw_ref[...]
        prev = jnp.where(i % tiles_per_seq == 0, 0.0, halo_ref[...])
        for rows in row_halves:
            u = project(rows) * hc_ref[rows].astype(f32)
            u_ext = jnp.concatenate([prev, u], axis=0)
            u_m1 = pltpu.roll(u_ext, 1, 0)[SUBLANES:]
            u_m2 = pltpu.roll(u_ext, 2, 0)[SUBLANES:]
            conv = u_m2 * cw[0:1] + u_m1 * cw[1:2]
            conv = conv + u * cw[2:3]
            y_pre = bb_ref[rows].astype(f32) * conv
            yp_ref[rows] = y_pre
            ss_ref[rows] = jnp.mean(y_pre * y_pre, axis=-1, keepdims=True)
            prev = u[-SUBLANES:]
        halo_ref[...] = prev

    @pl.when(j == SEG_ZC)
    def _():
        for rows in row_halves:
            z = project(rows)
            o_ref[0, rows] = (yp_ref[rows] * cg_ref[...] * silu(z)).astype(o_ref.dtype)
            rs_ref[rows] = jnp.broadcast_to(lax.rsqrt(ss_ref[rows] + EPS), (half, rs_ref.shape[1]))

    @pl.when(j == SEG_Q)
    def _():
        for rows in row_halves:
            acc = project(rows)
            for head in range(N_HEADS):
                cols = head_cols(head)
                normed = _rms_norm_f32(acc[:, cols], hg_ref[0, :, cols])
                qt_ref[head, :, rows] = normed.T.astype(qt_ref.dtype)

    @pl.when(j == SEG_K)
    def _():
        for rows in row_halves:
            acc = project(rows)
            for head in range(N_HEADS):
                cols = head_cols(head)
                normed = _rms_norm_f32(acc[:, cols], hg_ref[0, :, cols])
                o_ref[0, rows, cols] = normed.astype(o_ref.dtype)

    @pl.when(j == SEG_V)
    def _():
        acc = project()
        for head in range(N_HEADS):
            vt_ref[head] = acc[:, head_cols(head)].T.astype(vt_ref.dtype)

    @pl.when(j == SEG_ZA)
    def _():
        for rows in row_halves:
            o_ref[0, rows] = silu(project(rows)).astype(o_ref.dtype)


def _inproj(x2d, gain, w_bf16, head_gain, w_out, conv_w, conv_gain, seq):
    m, d = x2d.shape
    seg = w_bf16.shape[1] // N_SEG
    hd = seg // N_HEADS
    tm = IN_TILE_M
    n_tiles = m // tm
    wo_rows = w_out.shape[0] // (n_tiles * N_SEG)
    assert wo_rows * n_tiles * N_SEG == w_out.shape[0] and wo_rows % (2 * SUBLANES) == 0
    assert seq % tm == 0

    def slab(i, j):
        jj = (j >= SEG_K).astype(jnp.int32) + (j >= SEG_ZA).astype(jnp.int32)
        return (jj, i, 0)

    def wo_block(i, j):
        return (i * N_SEG + j, 0)

    def whole(shape):
        return pl.BlockSpec(shape, lambda i, j: (0,) * len(shape))

    heads_t = pl.BlockSpec((N_HEADS, hd, tm), lambda i, j: (0, 0, i))
    return pl.pallas_call(
        functools.partial(_inproj_kernel, seq // tm),
        grid=(n_tiles, N_SEG),
        in_specs=[
            pl.BlockSpec((tm, d), lambda i, j: (i, 0)),
            whole((1, d)),
            pl.BlockSpec((d, seg), lambda i, j: (0, j)),
            pl.BlockSpec((1, 1, seg), lambda i, j: (j, 0, 0)),
            pl.BlockSpec((wo_rows, w_out.shape[1]), wo_block),
            whole((CONV_WIDTH, seg)), whole((1, seg)),
        ],
        out_specs=[
            pl.BlockSpec((1, tm, seg), slab),
            heads_t, heads_t,
            pl.BlockSpec((wo_rows, w_out.shape[1]), wo_block),
            pl.BlockSpec((tm, LANES), lambda i, j: (i, 0)),
        ],
        out_shape=[
            jax.ShapeDtypeStruct((N_SLABS, m, seg), jnp.bfloat16),
            jax.ShapeDtypeStruct((N_HEADS, hd, m), jnp.bfloat16),
            jax.ShapeDtypeStruct((N_HEADS, hd, m), jnp.bfloat16),
            jax.ShapeDtypeStruct(w_out.shape, jnp.bfloat16),
            jax.ShapeDtypeStruct((m, LANES), jnp.float32),
        ],
        scratch_shapes=[
            pltpu.VMEM((tm, d), jnp.bfloat16),
            pltpu.VMEM((tm, seg), jnp.bfloat16),
            pltpu.VMEM((tm, seg), jnp.bfloat16),
            pltpu.VMEM((tm, seg), jnp.float32),
            pltpu.VMEM((tm, 1), jnp.float32),
            pltpu.VMEM((SUBLANES, seg), jnp.float32),
        ],
        compiler_params=pltpu.CompilerParams(
            dimension_semantics=("arbitrary", "arbitrary"),
            vmem_limit_bytes=V7X_VMEM_LIMIT_BYTES,
        ),
        name="inproj",
    )(x2d, gain.reshape(1, d), w_bf16, head_gain, w_out, conv_w, conv_gain.reshape(1, seg))


def _moba_kernel(par_ref, qt_ref, k_ref, vt_ref, o_ref):
    f32, bf16 = jnp.float32, jnp.bfloat16
    n_heads, hd, seq = qt_ref.shape
    blk = MOBA_BLOCK
    n_blocks = seq // blk
    assert n_blocks == SUBLANES and hd == LANES

    key_row = lax.broadcasted_iota(jnp.int32, (blk, blk), 0)
    qry_col = lax.broadcasted_iota(jnp.int32, (blk, blk), 1)
    causal = key_row <= qry_col
    blk_row = lax.broadcasted_iota(jnp.int32, (n_blocks, blk), 0)

    def head_cols(head):
        return slice(head * hd, (head + 1) * hd)

    def keys(head, n):
        return k_ref[0, 0, n * blk:(n + 1) * blk, head_cols(head)]

    def block_means(head):
        k_all = k_ref[0, 0, :, head_cols(head)].astype(f32)
        k_mean = jnp.mean(k_all.reshape(n_blocks, blk, hd), axis=1)
        km_hi = k_mean.astype(bf16)
        km_lo = (k_mean - km_hi.astype(f32)).astype(bf16)
        return jnp.concatenate([km_hi, km_lo], axis=0)

    km_hl = [block_means(head) for head in range(n_heads)]

    def block_bias(head, own, q_t):
        if own <= MOBA_TOPK:
            return None
        g2 = jnp.dot(km_hl[head], q_t, preferred_element_type=f32)
        gate = g2[:n_blocks] + g2[n_blocks:]
        rank = jnp.zeros(gate.shape, f32)
        for m in range(own):
            g_m = gate[m:m + 1, :]
            ahead = (g_m > gate) | ((g_m == gate) & (blk_row > m))
            rank = rank + jnp.where(ahead, 1.0, 0.0)
        return jnp.where((rank < MOBA_TOPK) | (blk_row >= own), 0.0, MASKED)

    def finish(head, own, probs, l):
        n_keys = (own + 1) * blk
        p_all = jnp.concatenate([p.astype(bf16) for p in probs], axis=0)
        acc = jnp.dot(vt_ref[head, :, 0:n_keys], p_all, preferred_element_type=f32)
        out_t = acc / l
        o_ref[0, own * blk:(own + 1) * blk, head_cols(head)] = out_t.T.astype(o_ref.dtype)

    def tile_exact(head, own):
        q_t = qt_ref[head, :, own * blk:(own + 1) * blk]
        bias = block_bias(head, own, q_t)
        parts = []
        for n in range(own + 1):
            part = jnp.dot(keys(head, n), q_t, preferred_element_type=f32)
            if n == own:
                part = jnp.where(causal, part, MASKED)
            elif bias is not None:
                part = part + bias[n:n + 1, :]
            parts.append(part)
        m = functools.reduce(jnp.maximum, [jnp.max(part, axis=0, keepdims=True) for part in parts])
        probs = [jnp.exp2(part - m) for part in parts]
        l = functools.reduce(jnp.add, [jnp.sum(p, axis=0, keepdims=True) for p in probs])
        finish(head, own, probs, l)

    def tile_bounded(head, own):
        q_t = qt_ref[head, :, own * blk:(own + 1) * blk]
        q_f = q_t.astype(f32)
        bound = jnp.sqrt(jnp.sum(q_f * q_f, axis=0, keepdims=True)) * par_ref[0]
        bias = block_bias(head, own, q_t)
        if bias is None:
            bias = jnp.zeros((n_blocks, blk), f32)
        shift = jnp.broadcast_to(-bound, (SUBLANES, blk))
        pad = jnp.zeros((hd - n_blocks - SUBLANES, blk), f32)
        extra = jnp.concatenate([bias, shift, pad], axis=0).astype(bf16)
        rhs = jnp.concatenate([q_t, extra], axis=0)
        lane = lax.broadcasted_iota(jnp.int32, (blk, hd), 1)
        probs = []
        for n in range(own + 1):
            pick = jnp.where((lane == n) | (lane == n_blocks), 1.0, 0.0).astype(bf16)
            lhs = jnp.concatenate([keys(head, n), pick], axis=1)
            part = jnp.dot(lhs, rhs, preferred_element_type=f32)
            if n == own:
                part = jnp.where(causal, part, MASKED)
            probs.append(jnp.exp2(part))
        l = functools.reduce(jnp.add, [jnp.sum(p, axis=0, keepdims=True) for p in probs])
        finish(head, own, probs, l)

    bounded_ok = par_ref[1] > 0.5

    @pl.when(bounded_ok)
    def _():
        for own in range(n_blocks):
            for head in range(n_heads):
                tile_bounded(head, own)

    @pl.when(jnp.logical_not(bounded_ok))
    def _():
        for own in range(n_blocks):
            for head in range(n_heads):
                tile_exact(head, own)


def _moba(proj4d, q_t, v_t, params):
    _, bsz, seq, seg = proj4d.shape
    hd = seg // N_HEADS
    group = MOBA_HEADS_PER_STEP
    heads_t = pl.BlockSpec((group, hd, seq), lambda b, g: (g, 0, b))
    return pl.pallas_call(
        _moba_kernel,
        grid=(bsz, N_HEADS // group),
        in_specs=[pl.BlockSpec(memory_space=pltpu.SMEM), heads_t,
                  pl.BlockSpec((1, 1, seq, group * hd), lambda b, g: (SLAB_K, b, 0, g)), heads_t],
        out_specs=pl.BlockSpec((1, seq, group * hd), lambda b, g: (b, 0, g)),
        out_shape=jax.ShapeDtypeStruct((bsz, seq, seg), jnp.bfloat16),
        compiler_params=pltpu.CompilerParams(
            dimension_semantics=("arbitrary", "arbitrary"),
            vmem_limit_bytes=V7X_VMEM_LIMIT_BYTES,
        ),
        name="moba",
    )(params, q_t, proj4d, v_t)


def _outproj_kernel(gc_ref, ga_ref, rsc_ref, a_ref, x_ref, ag_ref, w_ref, o_ref):
    f32 = jnp.float32
    tm = o_ref.shape[0]
    d_conv = gc_ref.shape[2]
    for r0 in range(0, tm, OUT_SLAB_M):
        rows = slice(r0, r0 + OUT_SLAB_M)
        attn = a_ref[rows].astype(f32)
        rs_attn = lax.rsqrt(jnp.mean(attn * attn, axis=-1, keepdims=True) + EPS)
        g_attn = (attn * ag_ref[...] * ga_ref[0, rows].astype(f32)).astype(jnp.bfloat16)
        p_conv = jnp.dot(gc_ref[0, rows], w_ref[:d_conv], preferred_element_type=f32)
        p_attn = jnp.dot(g_attn, w_ref[d_conv:], preferred_element_type=f32)
        o_ref[rows] = x_ref[rows] + (rsc_ref[rows, 0:1] * p_conv + rs_attn * p_attn)


def _outproj(slabs, rs_conv, attn2d, x2d, attn_gain, w_out_bf16):
    _, m, seg = slabs.shape
    d = x2d.shape[1]
    tm = OUT_TILE_M

    def slab_spec(slab):
        return pl.BlockSpec((1, tm, seg), lambda i: (slab, i, 0))

    def whole(shape):
        return pl.BlockSpec(shape, lambda i: (0,) * len(shape))

    return pl.pallas_call(
        _outproj_kernel,
        grid=(m // tm,),
        in_specs=[
            slab_spec(SLAB_CONV), slab_spec(SLAB_GATE),
            pl.BlockSpec((tm, LANES), lambda i: (i, 0)),
            pl.BlockSpec((tm, seg), lambda i: (i, 0)),
            pl.BlockSpec((tm, d), lambda i: (i, 0)),
            whole((1, seg)),
            whole(w_out_bf16.shape),
        ],
        out_specs=pl.BlockSpec((tm, d), lambda i: (i, 0)),
        out_shape=jax.ShapeDtypeStruct((m, d), jnp.float32),
        compiler_params=pltpu.CompilerParams(
            dimension_semantics=("arbitrary",),
            vmem_limit_bytes=V7X_VMEM_LIMIT_BYTES,
        ),
        name="outproj",
    )(slabs, slabs, rs_conv, attn2d, x2d, attn_gain.reshape(1, seg), w_out_bf16)


def _head_gains(q_gain, k_gain, seg):
    hd = q_gain.shape[0]
    score_scale = hd ** -0.5 * LOG2E
    rows = jnp.ones((N_SEG, seg), jnp.float32)
    rows = rows.at[SEG_Q].set(jnp.tile(q_gain * score_scale, N_HEADS))
    rows = rows.at[SEG_K].set(jnp.tile(k_gain, N_HEADS))
    return rows.reshape(N_SEG, 1, seg)


def _softmax_shift_params(q_gain, k_gain):
    hd = q_gain.shape[0]
    k_max = math.sqrt(hd) * jnp.max(jnp.abs(k_gain)) * BF16_NORM_SLACK
    q_max = math.sqrt(hd) * jnp.max(jnp.abs(q_gain)) * (hd ** -0.5 * LOG2E) * BF16_NORM_SLACK
    bounded_ok = 2.0 * q_max * k_max <= MAX_SHIFT_GAP
    return jnp.stack([k_max, bounded_ok.astype(jnp.float32)]).astype(jnp.float32)


def kernel(x, norm_gain, w_in, conv_w, q_norm_gain, k_norm_gain, conv_out_gain, attn_out_gain, w_out):
    bsz, seq, d = x.shape
    depth = norm_gain.shape[0]
    seg = w_in.shape[2] // N_SEG
    assert seq % MOBA_BLOCK == 0 and (bsz * seq) % OUT_TILE_M == 0 and (bsz * seq) % IN_TILE_M == 0
    assert w_out.shape[1] == 2 * seg and seg == N_HEADS * q_norm_gain.shape[1]
    x2d = x.reshape(bsz * seq, d)
    for layer in range(depth):
        slabs, q_t, v_t, w_out_bf16, rs_conv = _inproj(
            x2d, norm_gain[layer], w_in[layer].astype(jnp.bfloat16),
            _head_gains(q_norm_gain[layer], k_norm_gain[layer], seg), w_out[layer],
            conv_w[layer], conv_out_gain[layer], seq)
        attn = _moba(slabs.reshape(N_SLABS, bsz, seq, seg), q_t, v_t,
                     _softmax_shift_params(q_norm_gain[layer], k_norm_gain[layer]))
        x2d = _outproj(slabs, rs_conv, attn.reshape(bsz * seq, seg), x2d, attn_out_gain[layer],
                       w_out_bf16)
    return x2d.reshape(bsz, seq, d)
```

```python
import functools
import math

import jax
import jax.numpy as jnp
from jax import lax
from jax.experimental import pallas as pl
from jax.experimental.pallas import tpu as pltpu

N_HEADS = 8
CONV_WIDTH = 3
MOBA_BLOCK = 256
MOBA_TOPK = 3
EPS = 1e-6
N_SEG = 8
SEG_H, SEG_B, SEG_C, SEG_ZC, SEG_Q, SEG_K, SEG_V, SEG_ZA = range(N_SEG)
N_SLABS = 3
SLAB_CONV, SLAB_K, SLAB_GATE = range(N_SLABS)

MASKED = -1e30
LOG2E = math.log2(math.e)
MAX_SHIFT_GAP = 100.0
BF16_NORM_SLACK = 1.0 + 2.0 ** -6

V7X_VMEM_LIMIT_BYTES = 56 * 1024 * 1024
SUBLANES = 8
LANES = 128

IN_TILE_M = 1024
OUT_TILE_M = 512
OUT_SLAB_M = 256
MOBA_HEADS_PER_STEP = 1


def _rms_norm_f32(x, gain):
    return x * lax.rsqrt(jnp.mean(x * x, axis=-1, keepdims=True) + EPS) * gain


def _inproj_kernel(tiles_per_seq, x_ref, g_ref, w_ref, hg_ref, wo_ref, cw_ref, cg_ref,
                   o_ref, qt_ref, vt_ref, wob_ref, rs_ref,
                   h_ref, hc_ref, bb_ref, yp_ref, ss_ref, halo_ref):
    i, j = pl.program_id(0), pl.program_id(1)
    f32 = jnp.float32
    hd = qt_ref.shape[1]

    wob_ref[...] = wo_ref[...].astype(wob_ref.dtype)

    def project(rows=slice(None)):
        return jnp.dot(h_ref[rows], w_ref[...], preferred_element_type=f32)

    half = h_ref.shape[0] // 2
    row_halves = (slice(0, half), slice(half, 2 * half))

    def head_cols(head):
        return slice(head * hd, (head + 1) * hd)

    def silu(z):
        return z * jax.nn.sigmoid(z)

    @pl.when(j == SEG_H)
    def _():
        for rows in row_halves:
            h = _rms_norm_f32(x_ref[rows], g_ref[...]).astype(h_ref.dtype)
            h_ref[rows] = h
            hc_ref[rows] = jnp.dot(h, w_ref[...], preferred_element_type=f32).astype(hc_ref.dtype)

    @pl.when(j == SEG_B)
    def _():
        bb_ref[...] = project().astype(bb_ref.dtype)

    @pl.when(j == SEG_C)
    def _():
        for rows in row_halves:
            yp_ref[rows] = project(rows) * hc_ref[rows].astype(f32)

    @pl.when(j == SEG_ZC)
    def _():
        cw = cw_ref[...]
        prev = jnp.where(i % tiles_per_seq == 0, 0.0, halo_ref[...])
        for rows in row_halves:
            u = yp_ref[rows]
            u_ext = jnp.concatenate([prev, u], axis=0)
            u_m1 = pltpu.roll(u_ext, 1, 0)[SUBLANES:]
            u_m2 = pltpu.roll(u_ext, 2, 0)[SUBLANES:]
            conv = u_m2 * cw[0:1] + u_m1 * cw[1:2]
            conv = conv + u * cw[2:3]
            y_pre = bb_ref[rows].astype(f32) * conv
            ss = jnp.mean(y_pre * y_pre, axis=-1, keepdims=True)
            z = project(rows)
            o_ref[0, rows] = (y_pre * cg_ref[...] * silu(z)).astype(o_ref.dtype)
            rs_ref[rows] = jnp.broadcast_to(lax.rsqrt(ss + EPS), (half, rs_ref.shape[1]))
            prev = u[-SUBLANES:]
        halo_ref[...] = prev

    @pl.when(j == SEG_Q)
    def _():
        for rows in row_halves:
            acc = project(rows)
            for head in range(N_HEADS):
                cols = head_cols(head)
                normed = _rms_norm_f32(acc[:, cols], hg_ref[0, :, cols])
                qt_ref[head, :, rows] = normed.T.astype(qt_ref.dtype)

    @pl.when(j == SEG_K)
    def _():
        for rows in row_halves:
            acc = project(rows)
            for head in range(N_HEADS):
                cols = head_cols(head)
                normed = _rms_norm_f32(acc[:, cols], hg_ref[0, :, cols])
                o_ref[0, rows, cols] = normed.astype(o_ref.dtype)

    @pl.when(j == SEG_V)
    def _():
        acc = project()
        for head in range(N_HEADS):
            vt_ref[head] = acc[:, head_cols(head)].T.astype(vt_ref.dtype)

    @pl.when(j == SEG_ZA)
    def _():
        for rows in row_halves:
            o_ref[0, rows] = silu(project(rows)).astype(o_ref.dtype)


def _inproj(x2d, gain, w_bf16, head_gain, w_out, conv_w, conv_gain, seq):
    m, d = x2d.shape
    seg = w_bf16.shape[1] // N_SEG
    hd = seg // N_HEADS
    tm = IN_TILE_M
    n_tiles = m // tm
    wo_rows = w_out.shape[0] // (n_tiles * N_SEG)
    assert wo_rows * n_tiles * N_SEG == w_out.shape[0] and wo_rows % (2 * SUBLANES) == 0
    assert seq % tm == 0

    def slab(i, j):
        jj = (j >= SEG_K).astype(jnp.int32) + (j >= SEG_ZA).astype(jnp.int32)
        return (jj, i, 0)

    def wo_block(i, j):
        return (i * N_SEG + j, 0)

    def whole(shape):
        return pl.BlockSpec(shape, lambda i, j: (0,) * len(shape))

    heads_t = pl.BlockSpec((N_HEADS, hd, tm), lambda i, j: (0, 0, i))
    return pl.pallas_call(
        functools.partial(_inproj_kernel, seq // tm),
        grid=(n_tiles, N_SEG),
        in_specs=[
            pl.BlockSpec((tm, d), lambda i, j: (i, 0)),
            whole((1, d)),
            pl.BlockSpec((d, seg), lambda i, j: (0, j)),
            pl.BlockSpec((1, 1, seg), lambda i, j: (j, 0, 0)),
            pl.BlockSpec((wo_rows, w_out.shape[1]), wo_block),
            whole((CONV_WIDTH, seg)), whole((1, seg)),
        ],
        out_specs=[
            pl.BlockSpec((1, tm, seg), slab),
            heads_t, heads_t,
            pl.BlockSpec((wo_rows, w_out.shape[1]), wo_block),
            pl.BlockSpec((tm, LANES), lambda i, j: (i, 0)),
        ],
        out_shape=[
            jax.ShapeDtypeStruct((N_SLABS, m, seg), jnp.bfloat16),
            jax.ShapeDtypeStruct((N_HEADS, hd, m), jnp.bfloat16),
            jax.ShapeDtypeStruct((N_HEADS, hd, m), jnp.bfloat16),
            jax.ShapeDtypeStruct(w_out.shape, jnp.bfloat16),
            jax.ShapeDtypeStruct((m, LANES), jnp.float32),
        ],
        scratch_shapes=[
            pltpu.VMEM((tm, d), jnp.bfloat16),
            pltpu.VMEM((tm, seg), jnp.bfloat16),
            pltpu.VMEM((tm, seg), jnp.bfloat16),
            pltpu.VMEM((tm, seg), jnp.float32),
            pltpu.VMEM((tm, 1), jnp.float32),
            pltpu.VMEM((SUBLANES, seg), jnp.float32),
        ],
        compiler_params=pltpu.CompilerParams(
            dimension_semantics=("arbitrary", "arbitrary"),
            vmem_limit_bytes=V7X_VMEM_LIMIT_BYTES,
        ),
        name="inproj",
    )(x2d, gain.reshape(1, d), w_bf16, head_gain, w_out, conv_w, conv_gain.reshape(1, seg))


def _moba_kernel(par_ref, qt_ref, k_ref, vt_ref, o_ref):
    f32, bf16 = jnp.float32, jnp.bfloat16
    n_heads, hd, seq = qt_ref.shape
    blk = MOBA_BLOCK
    n_blocks = seq // blk
    assert n_blocks == SUBLANES and hd == LANES

    key_row = lax.broadcasted_iota(jnp.int32, (blk, blk), 0)
    qry_col = lax.broadcasted_iota(jnp.int32, (blk, blk), 1)
    causal = key_row <= qry_col
    blk_row = lax.broadcasted_iota(jnp.int32, (n_blocks, blk), 0)

    def head_cols(head):
        return slice(head * hd, (head + 1) * hd)

    def keys(head, n):
        return k_ref[0, 0, n * blk:(n + 1) * blk, head_cols(head)]

    def block_means(head):
        k_all = k_ref[0, 0, :, head_cols(head)].astype(f32)
        k_mean = jnp.mean(k_all.reshape(n_blocks, blk, hd), axis=1)
        km_hi = k_mean.astype(bf16)
        km_lo = (k_mean - km_hi.astype(f32)).astype(bf16)
        return jnp.concatenate([km_hi, km_lo], axis=0)

    km_hl = [block_means(head) for head in range(n_heads)]

    def block_bias(head, own, q_t):
        if own <= MOBA_TOPK:
            return None
        g2 = jnp.dot(km_hl[head], q_t, preferred_element_type=f32)
        gate = g2[:n_blocks] + g2[n_blocks:]
        rank = jnp.zeros(gate.shape, f32)
        for m in range(own):
            g_m = gate[m:m + 1, :]
            ahead = (g_m > gate) | ((g_m == gate) & (blk_row > m))
            rank = rank + jnp.where(ahead, 1.0, 0.0)
        return jnp.where((rank < MOBA_TOPK) | (blk_row >= own), 0.0, MASKED)

    def finish(head, own, probs, l):
        n_keys = (own + 1) * blk
        p_all = jnp.concatenate([p.astype(bf16) for p in probs], axis=0)
        acc = jnp.dot(vt_ref[head, :, 0:n_keys], p_all, preferred_element_type=f32)
        out_t = acc / l
        o_ref[0, own * blk:(own + 1) * blk, head_cols(head)] = out_t.T.astype(o_ref.dtype)

    def tile_exact(head, own):
        q_t = qt_ref[head, :, own * blk:(own + 1) * blk]
        bias = block_bias(head, own, q_t)
        parts = []
        for n in range(own + 1):
            part = jnp.dot(keys(head, n), q_t, preferred_element_type=f32)
            if n == own:
                part = jnp.where(causal, part, MASKED)
            elif bias is not None:
                part = part + bias[n:n + 1, :]
            parts.append(part)
        m = functools.reduce(jnp.maximum, [jnp.max(part, axis=0, keepdims=True) for part in parts])
        probs = [jnp.exp2(part - m) for part in parts]
        l = functools.reduce(jnp.add, [jnp.sum(p, axis=0, keepdims=True) for p in probs])
        finish(head, own, probs, l)

    def tile_bounded(head, own):
        q_t = qt_ref[head, :, own * blk:(own + 1) * blk]
        q_f = q_t.astype(f32)
        bound = jnp.sqrt(jnp.sum(q_f * q_f, axis=0, keepdims=True)) * par_ref[0]
        bias = block_bias(head, own, q_t)
        if bias is None:
            bias = jnp.zeros((n_blocks, blk), f32)
        shift = jnp.broadcast_to(-bound, (SUBLANES, blk))
        pad = jnp.zeros((hd - n_blocks - SUBLANES, blk), f32)
        extra = jnp.concatenate([bias, shift, pad], axis=0).astype(bf16)
        rhs = jnp.concatenate([q_t, extra], axis=0)
        lane = lax.broadcasted_iota(jnp.int32, (blk, hd), 1)
        probs = []
        for n in range(own + 1):
            pick = jnp.where((lane == n) | (lane == n_blocks), 1.0, 0.0).astype(bf16)
            lhs = jnp.concatenate([keys(head, n), pick], axis=1)
            part = jnp.dot(lhs, rhs, preferred_element_type=f32)
            if n == own:
                part = jnp.where(causal, part, MASKED)
            probs.append(jnp.exp2(part))
        l = functools.reduce(jnp.add, [jnp.sum(p, axis=0, keepdims=True) for p in probs])
        finish(head, own, probs, l)

    bounded_ok = par_ref[1] > 0.5

    @pl.when(bounded_ok)
    def _():
        for own in range(n_blocks):
            for head in range(n_heads):
                tile_bounded(head, own)

    @pl.when(jnp.logical_not(bounded_ok))
    def _():
        for own in range(n_blocks):
            for head in range(n_heads):
                tile_exact(head, own)


def _moba(proj4d, q_t, v_t, params):
    _, bsz, seq, seg = proj4d.shape
    hd = seg // N_HEADS
    group = MOBA_HEADS_PER_STEP
    heads_t = pl.BlockSpec((group, hd, seq), lambda b, g: (g, 0, b))
    return pl.pallas_call(
        _moba_kernel,
        grid=(bsz, N_HEADS // group),
        in_specs=[pl.BlockSpec(memory_space=pltpu.SMEM), heads_t,
                  pl.BlockSpec((1, 1, seq, group * hd), lambda b, g: (SLAB_K, b, 0, g)), heads_t],
        out_specs=pl.BlockSpec((1, seq, group * hd), lambda b, g: (b, 0, g)),
        out_shape=jax.ShapeDtypeStruct((bsz, seq, seg), jnp.bfloat16),
        compiler_params=pltpu.CompilerParams(
            dimension_semantics=("arbitrary", "arbitrary"),
            vmem_limit_bytes=V7X_VMEM_LIMIT_BYTES,
        ),
        name="moba",
    )(params, q_t, proj4d, v_t)


def _outproj_kernel(gc_ref, ga_ref, rsc_ref, a_ref, x_ref, ag_ref, w_ref, o_ref):
    f32 = jnp.float32
    tm = o_ref.shape[0]
    d_conv = gc_ref.shape[2]
    for r0 in range(0, tm, OUT_SLAB_M):
        rows = slice(r0, r0 + OUT_SLAB_M)
        attn = a_ref[rows].astype(f32)
        rs_attn = lax.rsqrt(jnp.mean(attn * attn, axis=-1, keepdims=True) + EPS)
        g_attn = (attn * ag_ref[...] * ga_ref[0, rows].astype(f32)).astype(jnp.bfloat16)
        p_conv = jnp.dot(gc_ref[0, rows], w_ref[:d_conv], preferred_element_type=f32)
        p_attn = jnp.dot(g_attn, w_ref[d_conv:], preferred_element_type=f32)
        o_ref[rows] = x_ref[rows] + (rsc_ref[rows, 0:1] * p_conv + rs_attn * p_attn)


def _outproj(slabs, rs_conv, attn2d, x2d, attn_gain, w_out_bf16):
    _, m, seg = slabs.shape
    d = x2d.shape[1]
    tm = OUT_TILE_M

    def slab_spec(slab):
        return pl.BlockSpec((1, tm, seg), lambda i: (slab, i, 0))

    def whole(shape):
        return pl.BlockSpec(shape, lambda i: (0,) * len(shape))

    return pl.pallas_call(
        _outproj_kernel,
        grid=(m // tm,),
        in_specs=[
            slab_spec(SLAB_CONV), slab_spec(SLAB_GATE),
            pl.BlockSpec((tm, LANES), lambda i: (i, 0)),
            pl.BlockSpec((tm, seg), lambda i: (i, 0)),
            pl.BlockSpec((tm, d), lambda i: (i, 0)),
            whole((1, seg)),
            whole(w_out_bf16.shape),
        ],
        out_specs=pl.BlockSpec((tm, d), lambda i: (i, 0)),
        out_shape=jax.ShapeDtypeStruct((m, d), jnp.float32),
        compiler_params=pltpu.CompilerParams(
            dimension_semantics=("arbitrary",),
            vmem_limit_bytes=V7X_VMEM_LIMIT_BYTES,
        ),
        name="outproj",
    )(slabs, slabs, rs_conv, attn2d, x2d, attn_gain.reshape(1, seg), w_out_bf16)


def _head_gains(q_gain, k_gain, seg):
    hd = q_gain.shape[0]
    score_scale = hd ** -0.5 * LOG2E
    rows = jnp.ones((N_SEG, seg), jnp.float32)
    rows = rows.at[SEG_Q].set(jnp.tile(q_gain * score_scale, N_HEADS))
    rows = rows.at[SEG_K].set(jnp.tile(k_gain, N_HEADS))
    return rows.reshape(N_SEG, 1, seg)


def _softmax_shift_params(q_gain, k_gain):
    hd = q_gain.shape[0]
    k_max = math.sqrt(hd) * jnp.max(jnp.abs(k_gain)) * BF16_NORM_SLACK
    q_max = math.sqrt(hd) * jnp.max(jnp.abs(q_gain)) * (hd ** -0.5 * LOG2E) * BF16_NORM_SLACK
    bounded_ok = 2.0 * q_max * k_max <= MAX_SHIFT_GAP
    return jnp.stack([k_max, bounded_ok.astype(jnp.float32)]).astype(jnp.float32)


def kernel(x, norm_gain, w_in, conv_w, q_norm_gain, k_norm_gain, conv_out_gain, attn_out_gain, w_out):
    bsz, seq, d = x.shape
    depth = norm_gain.shape[0]
    seg = w_in.shape[2] // N_SEG
    assert seq % MOBA_BLOCK == 0 and (bsz * seq) % OUT_TILE_M == 0 and (bsz * seq) % IN_TILE_M == 0
    assert w_out.shape[1] == 2 * seg and seg == N_HEADS * q_norm_gain.shape[1]
    x2d = x.reshape(bsz * seq, d)
    for layer in range(depth):
        slabs, q_t, v_t, w_out_bf16, rs_conv = _inproj(
            x2d, norm_gain[layer], w_in[layer].astype(jnp.bfloat16),
            _head_gains(q_norm_gain[layer], k_norm_gain[layer], seg), w_out[layer],
            conv_w[layer], conv_out_gain[layer], seq)
        attn = _moba(slabs.reshape(N_SLABS, bsz, seq, seg), q_t, v_t,
                     _softmax_shift_params(q_norm_gain[layer], k_norm_gain[layer]))
        x2d = _outproj(slabs, rs_conv, attn.reshape(bsz * seq, seg), x2d, attn_out_gain[layer],
                       w_out_bf16)
    return x2d.reshape(bsz, seq, d)
```

```python
import functools
import math

import jax
import jax.numpy as jnp
from jax import lax
from jax.experimental import pallas as pl
from jax.experimental.pallas import tpu as pltpu

N_HEADS = 8
CONV_WIDTH = 3
MOBA_BLOCK = 256
MOBA_TOPK = 3
EPS = 1e-6
N_SEG = 8
SEG_H, SEG_B, SEG_C, SEG_ZC, SEG_Q, SEG_K, SEG_V, SEG_ZA = range(N_SEG)
N_SLABS = 3
SLAB_CONV, SLAB_K, SLAB_GATE = range(N_SLABS)
STEP_C, STEP_B = SEG_B, SEG_C

MASKED = -1e30
LOG2E = math.log2(math.e)
MAX_SHIFT_GAP = 100.0
BF16_NORM_SLACK = 1.0 + 2.0 ** -6

V7X_VMEM_LIMIT_BYTES = 56 * 1024 * 1024
SUBLANES = 8
LANES = 128

IN_TILE_M = 1024
OUT_TILE_M = 512
OUT_SLAB_M = 256


def _rms_norm_f32(x, gain):
    return x * lax.rsqrt(jnp.mean(x * x, axis=-1, keepdims=True) + EPS) * gain


def _inproj_kernel(tiles_per_seq, x_ref, g_ref, w_ref, hg_ref, wo_ref, cw_ref, cg_ref, qg_ref,
                   o_ref, qt_ref, vt_ref, wob_ref, rs_ref,
                   h_ref, hc_ref, yp_ref, ss_ref, halo_ref):
    i, j = pl.program_id(0), pl.program_id(1)
    f32 = jnp.float32
    hd = qt_ref.shape[1]

    wob_ref[...] = wo_ref[...].astype(wob_ref.dtype)

    def project(rows=slice(None)):
        return jnp.dot(h_ref[rows], w_ref[...], preferred_element_type=f32)

    half = h_ref.shape[0] // 2
    row_halves = (slice(0, half), slice(half, 2 * half))

    def head_cols(head):
        return slice(head * hd, (head + 1) * hd)

    def silu(z):
        return z * jax.nn.sigmoid(z)

    @pl.when(j == SEG_H)
    def _():
        for rows in row_halves:
            h = _rms_norm_f32(x_ref[rows], g_ref[...]).astype(h_ref.dtype)
            h_ref[rows] = h
            hc_ref[rows] = jnp.dot(h, w_ref[...], preferred_element_type=f32).astype(hc_ref.dtype)

    @pl.when(j == STEP_C)
    def _():
        for rows in row_halves:
            yp_ref[rows] = project(rows) * hc_ref[rows].astype(f32)

    @pl.when(j == STEP_B)
    def _():
        cw = cw_ref[...]
        prev = jnp.where(i % tiles_per_seq == 0, 0.0, halo_ref[...])
        for rows in row_halves:
            u = yp_ref[rows]
            u_ext = jnp.concatenate([prev, u], axis=0)
            u_m1 = pltpu.roll(u_ext, 1, 0)[SUBLANES:]
            u_m2 = pltpu.roll(u_ext, 2, 0)[SUBLANES:]
            conv = u_m2 * cw[0:1] + u_m1 * cw[1:2]
            conv = conv + u * cw[2:3]
            y_pre = project(rows) * conv
            yp_ref[rows] = y_pre
            ss_ref[rows] = jnp.mean(y_pre * y_pre, axis=-1, keepdims=True)
            prev = u[-SUBLANES:]
        halo_ref[...] = prev

    @pl.when(j == SEG_ZC)
    def _():
        for rows in row_halves:
            z = project(rows)
            o_ref[0, rows] = (yp_ref[rows] * cg_ref[...] * silu(z)).astype(o_ref.dtype)
            rs_ref[rows] = jnp.broadcast_to(lax.rsqrt(ss_ref[rows] + EPS), (half, rs_ref.shape[1]))

    @pl.when(j == SEG_Q)
    def _():
        for rows in row_halves:
            acc = project(rows)
            for head in range(N_HEADS):
                q_t = acc[:, head_cols(head)].T
                rs = lax.rsqrt(jnp.mean(q_t * q_t, axis=0, keepdims=True) + EPS)
                qt_ref[head, :, rows] = (q_t * rs * qg_ref[...]).astype(qt_ref.dtype)

    @pl.when(j == SEG_K)
    def _():
        for rows in row_halves:
            acc = project(rows)
            for head in range(N_HEADS):
                cols = head_cols(head)
                normed = _rms_norm_f32(acc[:, cols], hg_ref[0, :, cols])
                o_ref[0, rows, cols] = normed.astype(o_ref.dtype)

    @pl.when(j == SEG_V)
    def _():
        acc = project()
        for head in range(N_HEADS):
            vt_ref[head] = acc[:, head_cols(head)].T.astype(vt_ref.dtype)

    @pl.when(j == SEG_ZA)
    def _():
        for rows in row_halves:
            o_ref[0, rows] = silu(project(rows)).astype(o_ref.dtype)


def _inproj(x2d, gain, w_bf16, head_gain, w_out, conv_w, conv_gain, seq):
    m, d = x2d.shape
    seg = w_bf16.shape[1] // N_SEG
    hd = seg // N_HEADS
    tm = IN_TILE_M
    n_tiles = m // tm
    wo_rows = w_out.shape[0] // (n_tiles * N_SEG)
    assert wo_rows * n_tiles * N_SEG == w_out.shape[0] and wo_rows % (2 * SUBLANES) == 0
    assert seq % tm == 0

    def slab(i, j):
        jj = (j >= SEG_K).astype(jnp.int32) + (j >= SEG_ZA).astype(jnp.int32)
        return (jj, i, 0)

    def w_block(i, j):
        swap = (j == STEP_C).astype(jnp.int32) - (j == STEP_B).astype(jnp.int32)
        return (0, j + swap)

    def wo_block(i, j):
        return (i * N_SEG + j, 0)

    def whole(shape):
        return pl.BlockSpec(shape, lambda i, j: (0,) * len(shape))

    heads_t = pl.BlockSpec((N_HEADS, hd, tm), lambda i, j: (0, 0, i))
    return pl.pallas_call(
        functools.partial(_inproj_kernel, seq // tm),
        grid=(n_tiles, N_SEG),
        in_specs=[
            pl.BlockSpec((tm, d), lambda i, j: (i, 0)),
            whole((1, d)),
            pl.BlockSpec((d, seg), w_block),
            pl.BlockSpec((1, 1, seg), lambda i, j: (j, 0, 0)),
            pl.BlockSpec((wo_rows, w_out.shape[1]), wo_block),
            whole((CONV_WIDTH, seg)), whole((1, seg)), whole((hd, 1)),
        ],
        out_specs=[
            pl.BlockSpec((1, tm, seg), slab),
            heads_t, heads_t,
            pl.BlockSpec((wo_rows, w_out.shape[1]), wo_block),
            pl.BlockSpec((tm, LANES), lambda i, j: (i, 0)),
        ],
        out_shape=[
            jax.ShapeDtypeStruct((N_SLABS, m, seg), jnp.bfloat16),
            jax.ShapeDtypeStruct((N_HEADS, hd, m), jnp.bfloat16),
            jax.ShapeDtypeStruct((N_HEADS, hd, m), jnp.bfloat16),
            jax.ShapeDtypeStruct(w_out.shape, jnp.bfloat16),
            jax.ShapeDtypeStruct((m, LANES), jnp.float32),
        ],
        scratch_shapes=[
            pltpu.VMEM((tm, d), jnp.bfloat16),
            pltpu.VMEM((tm, seg), jnp.bfloat16),
            pltpu.VMEM((tm, seg), jnp.float32),
            pltpu.VMEM((tm, 1), jnp.float32),
            pltpu.VMEM((SUBLANES, seg), jnp.float32),
        ],
        compiler_params=pltpu.CompilerParams(
            dimension_semantics=("arbitrary", "arbitrary"),
            vmem_limit_bytes=V7X_VMEM_LIMIT_BYTES,
        ),
        name="inproj",
    )(x2d, gain.reshape(1, d), w_bf16, head_gain, w_out, conv_w, conv_gain.reshape(1, seg),
      head_gain[SEG_Q, 0, :hd].reshape(hd, 1))


def _moba_kernel(par_ref, qt_ref, k_ref, vt_ref, o_ref):
    f32, bf16 = jnp.float32, jnp.bfloat16
    n_heads, hd, seq = qt_ref.shape
    blk = MOBA_BLOCK
    n_blocks = seq // blk
    assert n_blocks == SUBLANES and hd == LANES

    key_row = lax.broadcasted_iota(jnp.int32, (blk, blk), 0)
    qry_col = lax.broadcasted_iota(jnp.int32, (blk, blk), 1)
    causal = key_row <= qry_col
    blk_row = lax.broadcasted_iota(jnp.int32, (n_blocks, blk), 0)

    def head_cols(head):
        return slice(head * hd, (head + 1) * hd)

    def keys(head, n):
        return k_ref[0, 0, n * blk:(n + 1) * blk, head_cols(head)]

    def block_means(head):
        k_all = k_ref[0, 0, :, head_cols(head)].astype(f32)
        k_mean = jnp.mean(k_all.reshape(n_blocks, blk, hd), axis=1)
        km_hi = k_mean.astype(bf16)
        km_lo = (k_mean - km_hi.astype(f32)).astype(bf16)
        return jnp.concatenate([km_hi, km_lo], axis=0)

    km_hl = [block_means(head) for head in range(n_heads)]

    def block_bias(head, own, q_t):
        if own <= MOBA_TOPK:
            return None
        g2 = jnp.dot(km_hl[head], q_t, preferred_element_type=f32)
        gate = g2[:n_blocks] + g2[n_blocks:]
        rank = jnp.zeros(gate.shape, f32)
        for m in range(own):
            g_m = gate[m:m + 1, :]
            ahead = (g_m > gate) | ((g_m == gate) & (blk_row > m))
            rank = rank + jnp.where(ahead, 1.0, 0.0)
        return jnp.where((rank < MOBA_TOPK) | (blk_row >= own), 0.0, MASKED)

    def finish(head, own, probs, l):
        n_keys = (own + 1) * blk
        p_all = jnp.concatenate([p.astype(bf16) for p in probs], axis=0)
        acc = jnp.dot(vt_ref[head, :, 0:n_keys], p_all, preferred_element_type=f32)
        out_t = acc / l
        o_ref[0, own * blk:(own + 1) * blk, head_cols(head)] = out_t.T.astype(o_ref.dtype)

    def tile_exact(head, own):
        q_t = qt_ref[head, :, own * blk:(own + 1) * blk]
        bias = block_bias(head, own, q_t)
        parts = []
        for n in range(own + 1):
            part = jnp.dot(keys(head, n), q_t, preferred_element_type=f32)
            if n == own:
                part = jnp.where(causal, part, MASKED)
            elif bias is not None:
                part = part + bias[n:n + 1, :]
            parts.append(part)
        m = functools.reduce(jnp.maximum, [jnp.max(part, axis=0, keepdims=True) for part in parts])
        probs = [jnp.exp2(part - m) for part in parts]
        l = functools.reduce(jnp.add, [jnp.sum(p, axis=0, keepdims=True) for p in probs])
        finish(head, own, probs, l)

    def tile_bounded(head, own):
        q_t = qt_ref[head, :, own * blk:(own + 1) * blk]
        q_f = q_t.astype(f32)
        bound = jnp.sqrt(jnp.sum(q_f * q_f, axis=0, keepdims=True)) * par_ref[0]
        bias = block_bias(head, own, q_t)
        if bias is None:
            bias = jnp.zeros((n_blocks, blk), f32)
        shift = jnp.broadcast_to(-bound, (SUBLANES, blk))
        pad = jnp.zeros((hd - n_blocks - SUBLANES, blk), f32)
        extra = jnp.concatenate([bias, shift, pad], axis=0).astype(bf16)
        rhs = jnp.concatenate([q_t, extra], axis=0)
        lane = lax.broadcasted_iota(jnp.int32, (blk, hd), 1)
        probs = []
        for n in range(own + 1):
            pick = jnp.where((lane == n) | (lane == n_blocks), 1.0, 0.0).astype(bf16)
            lhs = jnp.concatenate([keys(head, n), pick], axis=1)
            part = jnp.dot(lhs, rhs, preferred_element_type=f32)
            if n == own:
                part = jnp.where(causal, part, MASKED)
            probs.append(jnp.exp2(part))
        l = functools.reduce(jnp.add, [jnp.sum(p, axis=0, keepdims=True) for p in probs])
        finish(head, own, probs, l)

    bounded_ok = par_ref[1] > 0.5

    @pl.when(bounded_ok)
    def _():
        for own in range(n_blocks):
            for head in range(n_heads):
                tile_bounded(head, own)

    @pl.when(jnp.logical_not(bounded_ok))
    def _():
        for own in range(n_blocks):
            for head in range(n_heads):
                tile_exact(head, own)


def _moba(proj4d, q_t, v_t, params):
    _, bsz, seq, seg = proj4d.shape
    hd = seg // N_HEADS
    heads_t = pl.BlockSpec((1, hd, seq), lambda b, h: (h, 0, b))
    return pl.pallas_call(
        _moba_kernel,
        grid=(bsz, N_HEADS),
        in_specs=[pl.BlockSpec(memory_space=pltpu.SMEM), heads_t,
                  pl.BlockSpec((1, 1, seq, hd), lambda b, h: (SLAB_K, b, 0, h)), heads_t],
        out_specs=pl.BlockSpec((1, seq, hd), lambda b, h: (b, 0, h)),
        out_shape=jax.ShapeDtypeStruct((bsz, seq, seg), jnp.bfloat16),
        compiler_params=pltpu.CompilerParams(
            dimension_semantics=("arbitrary", "arbitrary"),
            vmem_limit_bytes=V7X_VMEM_LIMIT_BYTES,
        ),
        name="moba",
    )(params, q_t, proj4d, v_t)


def _outproj_kernel(gc_ref, ga_ref, rsc_ref, a_ref, x_ref, ag_ref, w_ref, o_ref):
    f32 = jnp.float32
    tm = o_ref.shape[0]
    d_conv = gc_ref.shape[2]
    for r0 in range(0, tm, OUT_SLAB_M):
        rows = slice(r0, r0 + OUT_SLAB_M)
        attn = a_ref[rows].astype(f32)
        rs_attn = lax.rsqrt(jnp.mean(attn * attn, axis=-1, keepdims=True) + EPS)
        g_attn = (attn * ag_ref[...] * ga_ref[0, rows].astype(f32)).astype(jnp.bfloat16)
        p_conv = jnp.dot(gc_ref[0, rows], w_ref[:d_conv], preferred_element_type=f32)
        p_attn = jnp.dot(g_attn, w_ref[d_conv:], preferred_element_type=f32)
        o_ref[rows] = x_ref[rows] + (rsc_ref[rows, 0:1] * p_conv + rs_attn * p_attn)


def _outproj(slabs, rs_conv, attn2d, x2d, attn_gain, w_out_bf16):
    _, m, seg = slabs.shape
    d = x2d.shape[1]
    tm = OUT_TILE_M

    def slab_spec(slab):
        return pl.BlockSpec((1, tm, seg), lambda i: (slab, i, 0))

    def whole(shape):
        return pl.BlockSpec(shape, lambda i: (0,) * len(shape))

    return pl.pallas_call(
        _outproj_kernel,
        grid=(m // tm,),
        in_specs=[
            slab_spec(SLAB_CONV), slab_spec(SLAB_GATE),
            pl.BlockSpec((tm, LANES), lambda i: (i, 0)),
            pl.BlockSpec((tm, seg), lambda i: (i, 0)),
            pl.BlockSpec((tm, d), lambda i: (i, 0)),
            whole((1, seg)),
            whole(w_out_bf16.shape),
        ],
        out_specs=pl.BlockSpec((tm, d), lambda i: (i, 0)),
        out_shape=jax.ShapeDtypeStruct((m, d), jnp.float32),
        compiler_params=pltpu.CompilerParams(
            dimension_semantics=("arbitrary",),
            vmem_limit_bytes=V7X_VMEM_LIMIT_BYTES,
        ),
        name="outproj",
    )(slabs, slabs, rs_conv, attn2d, x2d, attn_gain.reshape(1, seg), w_out_bf16)


def _head_gains(q_gain, k_gain, seg):
    hd = q_gain.shape[0]
    score_scale = hd ** -0.5 * LOG2E
    rows = jnp.ones((N_SEG, seg), jnp.float32)
    rows = rows.at[SEG_Q].set(jnp.tile(q_gain * score_scale, N_HEADS))
    rows = rows.at[SEG_K].set(jnp.tile(k_gain, N_HEADS))
    return rows.reshape(N_SEG, 1, seg)


def _softmax_shift_params(q_gain, k_gain):
    hd = q_gain.shape[0]
    k_max = math.sqrt(hd) * jnp.max(jnp.abs(k_gain)) * BF16_NORM_SLACK
    q_max = math.sqrt(hd) * jnp.max(jnp.abs(q_gain)) * (hd ** -0.5 * LOG2E) * BF16_NORM_SLACK
    bounded_ok = 2.0 * q_max * k_max <= MAX_SHIFT_GAP
    return jnp.stack([k_max, bounded_ok.astype(jnp.float32)]).astype(jnp.float32)


def kernel(x, norm_gain, w_in, conv_w, q_norm_gain, k_norm_gain, conv_out_gain, attn_out_gain, w_out):
    bsz, seq, d = x.shape
    depth = norm_gain.shape[0]
    seg = w_in.shape[2] // N_SEG
    assert seq % MOBA_BLOCK == 0 and (bsz * seq) % OUT_TILE_M == 0 and (bsz * seq) % IN_TILE_M == 0
    assert w_out.shape[1] == 2 * seg and seg == N_HEADS * q_norm_gain.shape[1]
    x2d = x.reshape(bsz * seq, d)
    for layer in range(depth):
        slabs, q_t, v_t, w_out_bf16, rs_conv = _inproj(
            x2d, norm_gain[layer], w_in[layer].astype(jnp.bfloat16),
            _head_gains(q_norm_gain[layer], k_norm_gain[layer], seg), w_out[layer],
            conv_w[layer], conv_out_gain[layer], seq)
        attn = _moba(slabs.reshape(N_SLABS, bsz, seq, seg), q_t, v_t,
                     _softmax_shift_params(q_norm_gain[layer], k_norm_gain[layer]))
        x2d = _outproj(slabs, rs_conv, attn.reshape(bsz * seq, seg), x2d, attn_out_gain[layer],
                       w_out_bf16)
    return x2d.reshape(bsz, seq, d)
```

```python
import functools
import math

import jax
import jax.numpy as jnp
from jax import lax
from jax.experimental import pallas as pl
from jax.experimental.pallas import tpu as pltpu

N_HEADS = 8
CONV_WIDTH = 3
MOBA_BLOCK = 256
MOBA_TOPK = 3
EPS = 1e-6
N_SEG = 8
SEG_H, SEG_B, SEG_C, SEG_ZC, SEG_Q, SEG_K, SEG_V, SEG_ZA = range(N_SEG)
N_SLABS = 3
SLAB_CONV, SLAB_K, SLAB_GATE = range(N_SLABS)
STEP_C, STEP_B = SEG_B, SEG_C

MASKED = -1e30
LOG2E = math.log2(math.e)
MAX_SHIFT_GAP = 100.0
BF16_NORM_SLACK = 1.0 + 2.0 ** -6

V7X_VMEM_LIMIT_BYTES = 56 * 1024 * 1024
SUBLANES = 8
LANES = 128

IN_TILE_M = 1024
OUT_TILE_M = 512


def _rms_norm_f32(x, gain):
    return x * lax.rsqrt(jnp.mean(x * x, axis=-1, keepdims=True) + EPS) * gain


def _inproj_kernel(tiles_per_seq, x_ref, g_ref, w_ref, hg_ref, wo_ref, cw_ref, cg_ref, qg_ref,
                   o_ref, qt_ref, vt_ref, wob_ref,
                   h_ref, hc_ref, yp_ref, ss_ref, halo_ref):
    i, j = pl.program_id(0), pl.program_id(1)
    f32 = jnp.float32
    hd = qt_ref.shape[1]

    wob_ref[...] = wo_ref[...].astype(wob_ref.dtype)

    def project(rows=slice(None)):
        return jnp.dot(h_ref[rows], w_ref[...], preferred_element_type=f32)

    half = h_ref.shape[0] // 2
    row_halves = (slice(0, half), slice(half, 2 * half))

    def head_cols(head):
        return slice(head * hd, (head + 1) * hd)

    def silu(z):
        return z * jax.nn.sigmoid(z)

    @pl.when(j == SEG_H)
    def _():
        for rows in row_halves:
            h = _rms_norm_f32(x_ref[rows], g_ref[...]).astype(h_ref.dtype)
            h_ref[rows] = h
            hc_ref[rows] = jnp.dot(h, w_ref[...], preferred_element_type=f32).astype(hc_ref.dtype)

    @pl.when(j == STEP_C)
    def _():
        for rows in row_halves:
            yp_ref[rows] = project(rows) * hc_ref[rows].astype(f32)

    @pl.when(j == STEP_B)
    def _():
        cw = cw_ref[...]
        prev = jnp.where(i % tiles_per_seq == 0, 0.0, halo_ref[...])
        for rows in row_halves:
            u = yp_ref[rows]
            u_ext = jnp.concatenate([prev, u], axis=0)
            u_m1 = pltpu.roll(u_ext, 1, 0)[SUBLANES:]
            u_m2 = pltpu.roll(u_ext, 2, 0)[SUBLANES:]
            conv = u_m2 * cw[0:1] + u_m1 * cw[1:2]
            conv = conv + u * cw[2:3]
            y_pre = project(rows) * conv
            yp_ref[rows] = y_pre
            ss_ref[rows] = jnp.mean(y_pre * y_pre, axis=-1, keepdims=True)
            prev = u[-SUBLANES:]
        halo_ref[...] = prev

    @pl.when(j == SEG_ZC)
    def _():
        for rows in row_halves:
            z = project(rows)
            normed = yp_ref[rows] * lax.rsqrt(ss_ref[rows] + EPS) * cg_ref[...]
            o_ref[0, rows] = (normed * silu(z)).astype(o_ref.dtype)

    @pl.when(j == SEG_Q)
    def _():
        for rows in row_halves:
            acc = project(rows)
            for head in range(N_HEADS):
                q_t = acc[:, head_cols(head)].T
                rs = lax.rsqrt(jnp.mean(q_t * q_t, axis=0, keepdims=True) + EPS)
                qt_ref[head, :, rows] = (q_t * rs * qg_ref[...]).astype(qt_ref.dtype)

    @pl.when(j == SEG_K)
    def _():
        for rows in row_halves:
            acc = project(rows)
            for head in range(N_HEADS):
                cols = head_cols(head)
                normed = _rms_norm_f32(acc[:, cols], hg_ref[0, :, cols])
                o_ref[0, rows, cols] = normed.astype(o_ref.dtype)

    @pl.when(j == SEG_V)
    def _():
        acc = project()
        for head in range(N_HEADS):
            vt_ref[head] = acc[:, head_cols(head)].T.astype(vt_ref.dtype)

    @pl.when(j == SEG_ZA)
    def _():
        for rows in row_halves:
            o_ref[0, rows] = silu(project(rows)).astype(o_ref.dtype)


def _inproj(x2d, gain, w_bf16, head_gain, w_out, conv_w, conv_gain, seq):
    m, d = x2d.shape
    seg = w_bf16.shape[1] // N_SEG
    hd = seg // N_HEADS
    tm = IN_TILE_M
    n_tiles = m // tm
    wo_rows = w_out.shape[0] // (n_tiles * N_SEG)
    assert wo_rows * n_tiles * N_SEG == w_out.shape[0] and wo_rows % (2 * SUBLANES) == 0
    assert seq % tm == 0

    def slab(i, j):
        jj = (j >= SEG_K).astype(jnp.int32) + (j >= SEG_ZA).astype(jnp.int32)
        return (jj, i, 0)

    def w_block(i, j):
        swap = (j == STEP_C).astype(jnp.int32) - (j == STEP_B).astype(jnp.int32)
        return (0, j + swap)

    def wo_block(i, j):
        return (i * N_SEG + j, 0)

    def whole(shape):
        return pl.BlockSpec(shape, lambda i, j: (0,) * len(shape))

    heads_t = pl.BlockSpec((N_HEADS, hd, tm), lambda i, j: (0, 0, i))
    return pl.pallas_call(
        functools.partial(_inproj_kernel, seq // tm),
        grid=(n_tiles, N_SEG),
        in_specs=[
            pl.BlockSpec((tm, d), lambda i, j: (i, 0)),
            whole((1, d)),
            pl.BlockSpec((d, seg), w_block),
            pl.BlockSpec((1, 1, seg), lambda i, j: (j, 0, 0)),
            pl.BlockSpec((wo_rows, w_out.shape[1]), wo_block),
            whole((CONV_WIDTH, seg)), whole((1, seg)), whole((hd, 1)),
        ],
        out_specs=[
            pl.BlockSpec((1, tm, seg), slab),
            heads_t, heads_t,
            pl.BlockSpec((wo_rows, w_out.shape[1]), wo_block),
        ],
        out_shape=[
            jax.ShapeDtypeStruct((N_SLABS, m, seg), jnp.bfloat16),
            jax.ShapeDtypeStruct((N_HEADS, hd, m), jnp.bfloat16),
            jax.ShapeDtypeStruct((N_HEADS, hd, m), jnp.bfloat16),
            jax.ShapeDtypeStruct(w_out.shape, jnp.bfloat16),
        ],
        scratch_shapes=[
            pltpu.VMEM((tm, d), jnp.bfloat16),
            pltpu.VMEM((tm, seg), jnp.bfloat16),
            pltpu.VMEM((tm, seg), jnp.float32),
            pltpu.VMEM((tm, 1), jnp.float32),
            pltpu.VMEM((SUBLANES, seg), jnp.float32),
        ],
        compiler_params=pltpu.CompilerParams(
            dimension_semantics=("arbitrary", "arbitrary"),
            vmem_limit_bytes=V7X_VMEM_LIMIT_BYTES,
        ),
        name="inproj",
    )(x2d, gain.reshape(1, d), w_bf16, head_gain, w_out, conv_w, conv_gain.reshape(1, seg),
      head_gain[SEG_Q, 0, :hd].reshape(hd, 1))


def _moba_kernel(par_ref, qt_ref, k_ref, vt_ref, o_ref):
    f32, bf16 = jnp.float32, jnp.bfloat16
    n_heads, hd, seq = qt_ref.shape
    blk = MOBA_BLOCK
    n_blocks = seq // blk
    assert n_blocks == SUBLANES and hd == LANES

    key_row = lax.broadcasted_iota(jnp.int32, (blk, blk), 0)
    qry_col = lax.broadcasted_iota(jnp.int32, (blk, blk), 1)
    causal = key_row <= qry_col
    blk_row = lax.broadcasted_iota(jnp.int32, (n_blocks, blk), 0)

    def head_cols(head):
        return slice(head * hd, (head + 1) * hd)

    def keys(head, n):
        return k_ref[0, 0, n * blk:(n + 1) * blk, head_cols(head)]

    def block_means(head):
        k_all = k_ref[0, 0, :, head_cols(head)].astype(f32)
        k_mean = jnp.mean(k_all.reshape(n_blocks, blk, hd), axis=1)
        km_hi = k_mean.astype(bf16)
        km_lo = (k_mean - km_hi.astype(f32)).astype(bf16)
        return jnp.concatenate([km_hi, km_lo], axis=0)

    km_hl = [block_means(head) for head in range(n_heads)]

    def block_bias(head, own, q_t):
        if own <= MOBA_TOPK:
            return None
        g2 = jnp.dot(km_hl[head], q_t, preferred_element_type=f32)
        gate = g2[:n_blocks] + g2[n_blocks:]
        rank = jnp.zeros(gate.shape, f32)
        for m in range(own):
            g_m = gate[m:m + 1, :]
            ahead = (g_m > gate) | ((g_m == gate) & (blk_row > m))
            rank = rank + jnp.where(ahead, 1.0, 0.0)
        return jnp.where((rank < MOBA_TOPK) | (blk_row >= own), 0.0, MASKED)

    def finish(head, own, probs, l):
        n_keys = (own + 1) * blk
        p_all = jnp.concatenate([p.astype(bf16) for p in probs], axis=0)
        acc = jnp.dot(vt_ref[head, :, 0:n_keys], p_all, preferred_element_type=f32)
        out_t = acc / l
        o_ref[0, own * blk:(own + 1) * blk, head_cols(head)] = out_t.T.astype(o_ref.dtype)

    def tile_exact(head, own):
        q_t = qt_ref[head, :, own * blk:(own + 1) * blk]
        bias = block_bias(head, own, q_t)
        parts = []
        for n in range(own + 1):
            part = jnp.dot(keys(head, n), q_t, preferred_element_type=f32)
            if n == own:
                part = jnp.where(causal, part, MASKED)
            elif bias is not None:
                part = part + bias[n:n + 1, :]
            parts.append(part)
        m = functools.reduce(jnp.maximum, [jnp.max(part, axis=0, keepdims=True) for part in parts])
        probs = [jnp.exp2(part - m) for part in parts]
        l = functools.reduce(jnp.add, [jnp.sum(p, axis=0, keepdims=True) for p in probs])
        finish(head, own, probs, l)

    def tile_bounded(head, own):
        q_t = qt_ref[head, :, own * blk:(own + 1) * blk]
        q_f = q_t.astype(f32)
        bound = jnp.sqrt(jnp.sum(q_f * q_f, axis=0, keepdims=True)) * par_ref[0]
        bias = block_bias(head, own, q_t)
        if bias is None:
            bias = jnp.zeros((n_blocks, blk), f32)
        shift = jnp.broadcast_to(-bound, (SUBLANES, blk))
        pad = jnp.zeros((hd - n_blocks - SUBLANES, blk), f32)
        extra = jnp.concatenate([bias, shift, pad], axis=0).astype(bf16)
        rhs = jnp.concatenate([q_t, extra], axis=0)
        lane = lax.broadcasted_iota(jnp.int32, (blk, hd), 1)
        probs = []
        for n in range(own + 1):
            pick = jnp.where((lane == n) | (lane == n_blocks), 1.0, 0.0).astype(bf16)
            lhs = jnp.concatenate([keys(head, n), pick], axis=1)
            part = jnp.dot(lhs, rhs, preferred_element_type=f32)
            if n == own:
                part = jnp.where(causal, part, MASKED)
            probs.append(jnp.exp2(part))
        l = functools.reduce(jnp.add, [jnp.sum(p, axis=0, keepdims=True) for p in probs])
        finish(head, own, probs, l)

    bounded_ok = par_ref[1] > 0.5

    @pl.when(bounded_ok)
    def _():
        for own in range(n_blocks):
            for head in range(n_heads):
                tile_bounded(head, own)

    @pl.when(jnp.logical_not(bounded_ok))
    def _():
        for own in range(n_blocks):
            for head in range(n_heads):
                tile_exact(head, own)


def _moba(proj4d, q_t, v_t, params):
    _, bsz, seq, seg = proj4d.shape
    hd = seg // N_HEADS
    heads_t = pl.BlockSpec((1, hd, seq), lambda b, h: (h, 0, b))
    return pl.pallas_call(
        _moba_kernel,
        grid=(bsz, N_HEADS),
        in_specs=[pl.BlockSpec(memory_space=pltpu.SMEM), heads_t,
                  pl.BlockSpec((1, 1, seq, hd), lambda b, h: (SLAB_K, b, 0, h)), heads_t],
        out_specs=pl.BlockSpec((1, seq, hd), lambda b, h: (b, 0, h)),
        out_shape=jax.ShapeDtypeStruct((bsz, seq, seg), jnp.bfloat16),
        compiler_params=pltpu.CompilerParams(
            dimension_semantics=("arbitrary", "arbitrary"),
            vmem_limit_bytes=V7X_VMEM_LIMIT_BYTES,
        ),
        name="moba",
    )(params, q_t, proj4d, v_t)


def _outproj_kernel(yc_ref, ga_ref, a_ref, x_ref, ag_ref, w_ref, o_ref):
    f32 = jnp.float32
    y_attn = _rms_norm_f32(a_ref[...].astype(f32), ag_ref[...]) * ga_ref[0].astype(f32)
    y = jnp.concatenate([yc_ref[0], y_attn.astype(jnp.bfloat16)], axis=1)
    o_ref[...] = x_ref[...] + jnp.dot(y, w_ref[...], preferred_element_type=f32)


def _outproj(slabs, attn2d, x2d, attn_gain, w_out_bf16):
    _, m, seg = slabs.shape
    d = x2d.shape[1]
    tm = OUT_TILE_M

    def slab_spec(slab):
        return pl.BlockSpec((1, tm, seg), lambda i: (slab, i, 0))

    def whole(shape):
        return pl.BlockSpec(shape, lambda i: (0,) * len(shape))

    return pl.pallas_call(
        _outproj_kernel,
        grid=(m // tm,),
        in_specs=[
            slab_spec(SLAB_CONV), slab_spec(SLAB_GATE),
            pl.BlockSpec((tm, seg), lambda i: (i, 0)),
            pl.BlockSpec((tm, d), lambda i: (i, 0)),
            whole((1, seg)),
            whole(w_out_bf16.shape),
        ],
        out_specs=pl.BlockSpec((tm, d), lambda i: (i, 0)),
        out_shape=jax.ShapeDtypeStruct((m, d), jnp.float32),
        compiler_params=pltpu.CompilerParams(
            dimension_semantics=("arbitrary",),
            vmem_limit_bytes=V7X_VMEM_LIMIT_BYTES,
        ),
        name="outproj",
    )(slabs, slabs, attn2d, x2d, attn_gain.reshape(1, seg), w_out_bf16)


def _head_gains(q_gain, k_gain, seg):
    hd = q_gain.shape[0]
    score_scale = hd ** -0.5 * LOG2E
    rows = jnp.ones((N_SEG, seg), jnp.float32)
    rows = rows.at[SEG_Q].set(jnp.tile(q_gain * score_scale, N_HEADS))
    rows = rows.at[SEG_K].set(jnp.tile(k_gain, N_HEADS))
    return rows.reshape(N_SEG, 1, seg)


def _softmax_shift_params(q_gain, k_gain):
    hd = q_gain.shape[0]
    k_max = math.sqrt(hd) * jnp.max(jnp.abs(k_gain)) * BF16_NORM_SLACK
    q_max = math.sqrt(hd) * jnp.max(jnp.abs(q_gain)) * (hd ** -0.5 * LOG2E) * BF16_NORM_SLACK
    bounded_ok = 2.0 * q_max * k_max <= MAX_SHIFT_GAP
    return jnp.stack([k_max, bounded_ok.astype(jnp.float32)]).astype(jnp.float32)


def kernel(x, norm_gain, w_in, conv_w, q_norm_gain, k_norm_gain, conv_out_gain, attn_out_gain, w_out):
    bsz, seq, d = x.shape
    depth = norm_gain.shape[0]
    seg = w_in.shape[2] // N_SEG
    assert seq % MOBA_BLOCK == 0 and (bsz * seq) % OUT_TILE_M == 0 and (bsz * seq) % IN_TILE_M == 0
    assert w_out.shape[1] == 2 * seg and seg == N_HEADS * q_norm_gain.shape[1]
    x2d = x.reshape(bsz * seq, d)
    for layer in range(depth):
        slabs, q_t, v_t, w_out_bf16 = _inproj(
            x2d, norm_gain[layer], w_in[layer].astype(jnp.bfloat16),
            _head_gains(q_norm_gain[layer], k_norm_gain[layer], seg), w_out[layer],
            conv_w[layer], conv_out_gain[layer], seq)
        attn = _moba(slabs.reshape(N_SLABS, bsz, seq, seg), q_t, v_t,
                     _softmax_shift_params(q_norm_gain[layer], k_norm_gain[layer]))
        x2d = _outproj(slabs, attn.reshape(bsz * seq, seg), x2d, attn_out_gain[layer], w_out_bf16)
    return x2d.reshape(bsz, seq, d)
```

```python
import functools
import math

import jax
import jax.numpy as jnp
from jax import lax
from jax.experimental import pallas as pl
from jax.experimental.pallas import tpu as pltpu

N_HEADS = 8
CONV_WIDTH = 3
MOBA_BLOCK = 256
MOBA_TOPK = 3
EPS = 1e-6
N_SEG = 8
SEG_H, SEG_B, SEG_C, SEG_ZC, SEG_Q, SEG_K, SEG_V, SEG_ZA = range(N_SEG)
N_SLABS = 3
SLAB_CONV, SLAB_K, SLAB_GATE = range(N_SLABS)
STEP_C, STEP_B = SEG_B, SEG_C

MASKED = -1e30
LOG2E = math.log2(math.e)
MAX_SHIFT_GAP = 100.0
BF16_NORM_SLACK = 1.0 + 2.0 ** -6

V7X_VMEM_LIMIT_BYTES = 56 * 1024 * 1024
SUBLANES = 8
LANES = 128

IN_TILE_M = 1024
OUT_TILE_M = 512


def _rms_norm_f32(x, gain):
    return x * lax.rsqrt(jnp.mean(x * x, axis=-1, keepdims=True) + EPS) * gain


def _inproj_kernel(tiles_per_seq, x_ref, g_ref, w_ref, hg_ref, wo_ref, cw_ref, cg_ref, qg_ref,
                   o_ref, qt_ref, vt_ref, wob_ref, km_ref,
                   h_ref, hc_ref, yp_ref, ss_ref, halo_ref):
    i, j = pl.program_id(0), pl.program_id(1)
    f32 = jnp.float32
    hd = qt_ref.shape[1]

    wob_ref[...] = wo_ref[...].astype(wob_ref.dtype)

    def project(rows=slice(None)):
        return jnp.dot(h_ref[rows], w_ref[...], preferred_element_type=f32)

    half = h_ref.shape[0] // 2
    row_halves = (slice(0, half), slice(half, 2 * half))

    def head_cols(head):
        return slice(head * hd, (head + 1) * hd)

    def silu(z):
        return z * jax.nn.sigmoid(z)

    @pl.when(j == SEG_H)
    def _():
        for rows in row_halves:
            h = _rms_norm_f32(x_ref[rows], g_ref[...]).astype(h_ref.dtype)
            h_ref[rows] = h
            hc_ref[rows] = jnp.dot(h, w_ref[...], preferred_element_type=f32).astype(hc_ref.dtype)

    @pl.when(j == STEP_C)
    def _():
        for rows in row_halves:
            yp_ref[rows] = project(rows) * hc_ref[rows].astype(f32)

    @pl.when(j == STEP_B)
    def _():
        cw = cw_ref[...]
        prev = jnp.where(i % tiles_per_seq == 0, 0.0, halo_ref[...])
        for rows in row_halves:
            u = yp_ref[rows]
            u_ext = jnp.concatenate([prev, u], axis=0)
            u_m1 = pltpu.roll(u_ext, 1, 0)[SUBLANES:]
            u_m2 = pltpu.roll(u_ext, 2, 0)[SUBLANES:]
            conv = u_m2 * cw[0:1] + u_m1 * cw[1:2]
            conv = conv + u * cw[2:3]
            y_pre = project(rows) * conv
            yp_ref[rows] = y_pre
            ss_ref[rows] = jnp.mean(y_pre * y_pre, axis=-1, keepdims=True)
            prev = u[-SUBLANES:]
        halo_ref[...] = prev

    @pl.when(j == SEG_ZC)
    def _():
        for rows in row_halves:
            z = project(rows)
            normed = yp_ref[rows] * lax.rsqrt(ss_ref[rows] + EPS) * cg_ref[...]
            o_ref[0, rows] = (normed * silu(z)).astype(o_ref.dtype)

    @pl.when(j == SEG_Q)
    def _():
        for rows in row_halves:
            acc = project(rows)
            for head in range(N_HEADS):
                q_t = acc[:, head_cols(head)].T
                rs = lax.rsqrt(jnp.mean(q_t * q_t, axis=0, keepdims=True) + EPS)
                qt_ref[head, :, rows] = (q_t * rs * qg_ref[...]).astype(qt_ref.dtype)

    @pl.when(j == SEG_K)
    def _():
        for rows in row_halves:
            acc = project(rows)
            blocks = slice(rows.start // MOBA_BLOCK, rows.stop // MOBA_BLOCK)
            for head in range(N_HEADS):
                cols = head_cols(head)
                normed = _rms_norm_f32(acc[:, cols], hg_ref[0, :, cols])
                o_ref[0, rows, cols] = normed.astype(o_ref.dtype)
                km_ref[0, blocks, cols] = jnp.mean(normed.reshape(-1, MOBA_BLOCK, hd), axis=1)

    @pl.when(j == SEG_V)
    def _():
        acc = project()
        for head in range(N_HEADS):
            vt_ref[head] = acc[:, head_cols(head)].T.astype(vt_ref.dtype)

    @pl.when(j == SEG_ZA)
    def _():
        for rows in row_halves:
            o_ref[0, rows] = silu(project(rows)).astype(o_ref.dtype)


def _inproj(x2d, gain, w_bf16, head_gain, w_out, conv_w, conv_gain, seq):
    assert IN_TILE_M % MOBA_BLOCK == 0
    m, d = x2d.shape
    seg = w_bf16.shape[1] // N_SEG
    hd = seg // N_HEADS
    tm = IN_TILE_M
    n_tiles = m // tm
    wo_rows = w_out.shape[0] // (n_tiles * N_SEG)
    assert wo_rows * n_tiles * N_SEG == w_out.shape[0] and wo_rows % (2 * SUBLANES) == 0
    assert seq % tm == 0

    def slab(i, j):
        jj = (j >= SEG_K).astype(jnp.int32) + (j >= SEG_ZA).astype(jnp.int32)
        return (jj, i, 0)

    def w_block(i, j):
        swap = (j == STEP_C).astype(jnp.int32) - (j == STEP_B).astype(jnp.int32)
        return (0, j + swap)

    def wo_block(i, j):
        return (i * N_SEG + j, 0)

    def whole(shape):
        return pl.BlockSpec(shape, lambda i, j: (0,) * len(shape))

    heads_t = pl.BlockSpec((N_HEADS, hd, tm), lambda i, j: (0, 0, i))
    return pl.pallas_call(
        functools.partial(_inproj_kernel, seq // tm),
        grid=(n_tiles, N_SEG),
        in_specs=[
            pl.BlockSpec((tm, d), lambda i, j: (i, 0)),
            whole((1, d)),
            pl.BlockSpec((d, seg), w_block),
            pl.BlockSpec((1, 1, seg), lambda i, j: (j, 0, 0)),
            pl.BlockSpec((wo_rows, w_out.shape[1]), wo_block),
            whole((CONV_WIDTH, seg)), whole((1, seg)), whole((hd, 1)),
        ],
        out_specs=[
            pl.BlockSpec((1, tm, seg), slab),
            heads_t, heads_t,
            pl.BlockSpec((wo_rows, w_out.shape[1]), wo_block),
            pl.BlockSpec((1, tm // MOBA_BLOCK, seg), lambda i, j: (i, 0, 0)),
        ],
        out_shape=[
            jax.ShapeDtypeStruct((N_SLABS, m, seg), jnp.bfloat16),
            jax.ShapeDtypeStruct((N_HEADS, hd, m), jnp.bfloat16),
            jax.ShapeDtypeStruct((N_HEADS, hd, m), jnp.bfloat16),
            jax.ShapeDtypeStruct(w_out.shape, jnp.bfloat16),
            jax.ShapeDtypeStruct((n_tiles, tm // MOBA_BLOCK, seg), jnp.float32),
        ],
        scratch_shapes=[
            pltpu.VMEM((tm, d), jnp.bfloat16),
            pltpu.VMEM((tm, seg), jnp.bfloat16),
            pltpu.VMEM((tm, seg), jnp.float32),
            pltpu.VMEM((tm, 1), jnp.float32),
            pltpu.VMEM((SUBLANES, seg), jnp.float32),
        ],
        compiler_params=pltpu.CompilerParams(
            dimension_semantics=("arbitrary", "arbitrary"),
            vmem_limit_bytes=V7X_VMEM_LIMIT_BYTES,
        ),
        name="inproj",
    )(x2d, gain.reshape(1, d), w_bf16, head_gain, w_out, conv_w, conv_gain.reshape(1, seg),
      head_gain[SEG_Q, 0, :hd].reshape(hd, 1))


def _moba_kernel(par_ref, qt_ref, k_ref, vt_ref, km_ref, o_ref):
    f32, bf16 = jnp.float32, jnp.bfloat16
    n_heads, hd, seq = qt_ref.shape
    blk = MOBA_BLOCK
    n_blocks = seq // blk
    assert n_blocks == SUBLANES and hd == LANES

    key_row = lax.broadcasted_iota(jnp.int32, (blk, blk), 0)
    qry_col = lax.broadcasted_iota(jnp.int32, (blk, blk), 1)
    causal = key_row <= qry_col
    blk_row = lax.broadcasted_iota(jnp.int32, (n_blocks, blk), 0)

    def head_cols(head):
        return slice(head * hd, (head + 1) * hd)

    def keys(head, n):
        return k_ref[0, 0, n * blk:(n + 1) * blk, head_cols(head)]

    def block_means(head):
        k_mean = km_ref[0, :, head_cols(head)]
        km_hi = k_mean.astype(bf16)
        km_lo = (k_mean - km_hi.astype(f32)).astype(bf16)
        return jnp.concatenate([km_hi, km_lo], axis=0)

    km_hl = [block_means(head) for head in range(n_heads)]

    def block_bias(head, own, q_t):
        if own <= MOBA_TOPK:
            return None
        g2 = jnp.dot(km_hl[head], q_t, preferred_element_type=f32)
        gate = g2[:n_blocks] + g2[n_blocks:]
        rank = jnp.zeros(gate.shape, f32)
        for m in range(own):
            g_m = gate[m:m + 1, :]
            ahead = (g_m > gate) | ((g_m == gate) & (blk_row > m))
            rank = rank + jnp.where(ahead, 1.0, 0.0)
        return jnp.where((rank < MOBA_TOPK) | (blk_row >= own), 0.0, MASKED)

    def finish(head, own, probs, l):
        n_keys = (own + 1) * blk
        p_all = jnp.concatenate([p.astype(bf16) for p in probs], axis=0)
        acc = jnp.dot(vt_ref[head, :, 0:n_keys], p_all, preferred_element_type=f32)
        out_t = acc / l
        o_ref[0, own * blk:(own + 1) * blk, head_cols(head)] = out_t.T.astype(o_ref.dtype)

    def tile_exact(head, own):
        q_t = qt_ref[head, :, own * blk:(own + 1) * blk]
        bias = block_bias(head, own, q_t)
        parts = []
        for n in range(own + 1):
            part = jnp.dot(keys(head, n), q_t, preferred_element_type=f32)
            if n == own:
                part = jnp.where(causal, part, MASKED)
            elif bias is not None:
                part = part + bias[n:n + 1, :]
            parts.append(part)
        m = functools.reduce(jnp.maximum, [jnp.max(part, axis=0, keepdims=True) for part in parts])
        probs = [jnp.exp2(part - m) for part in parts]
        l = functools.reduce(jnp.add, [jnp.sum(p, axis=0, keepdims=True) for p in probs])
        finish(head, own, probs, l)

    def tile_bounded(head, own):
        q_t = qt_ref[head, :, own * blk:(own + 1) * blk]
        q_f = q_t.astype(f32)
        bound = jnp.sqrt(jnp.sum(q_f * q_f, axis=0, keepdims=True)) * par_ref[0]
        bias = block_bias(head, own, q_t)
        if bias is None:
            bias = jnp.zeros((n_blocks, blk), f32)
        shift = jnp.broadcast_to(-bound, (SUBLANES, blk))
        pad = jnp.zeros((hd - n_blocks - SUBLANES, blk), f32)
        extra = jnp.concatenate([bias, shift, pad], axis=0).astype(bf16)
        rhs = jnp.concatenate([q_t, extra], axis=0)
        lane = lax.broadcasted_iota(jnp.int32, (blk, hd), 1)
        probs = []
        for n in range(own + 1):
            pick = jnp.where((lane == n) | (lane == n_blocks), 1.0, 0.0).astype(bf16)
            lhs = jnp.concatenate([keys(head, n), pick], axis=1)
            part = jnp.dot(lhs, rhs, preferred_element_type=f32)
            if n == own:
                part = jnp.where(causal, part, MASKED)
            probs.append(jnp.exp2(part))
        l = functools.reduce(jnp.add, [jnp.sum(p, axis=0, keepdims=True) for p in probs])
        finish(head, own, probs, l)

    bounded_ok = par_ref[1] > 0.5

    @pl.when(bounded_ok)
    def _():
        for own in range(n_blocks):
            for head in range(n_heads):
                tile_bounded(head, own)

    @pl.when(jnp.logical_not(bounded_ok))
    def _():
        for own in range(n_blocks):
            for head in range(n_heads):
                tile_exact(head, own)


def _moba(proj4d, q_t, v_t, k_means, params):
    _, bsz, seq, seg = proj4d.shape
    hd = seg // N_HEADS
    heads_t = pl.BlockSpec((1, hd, seq), lambda b, h: (h, 0, b))
    return pl.pallas_call(
        _moba_kernel,
        grid=(bsz, N_HEADS),
        in_specs=[pl.BlockSpec(memory_space=pltpu.SMEM), heads_t,
                  pl.BlockSpec((1, 1, seq, hd), lambda b, h: (SLAB_K, b, 0, h)), heads_t,
                  pl.BlockSpec((1, seq // MOBA_BLOCK, hd), lambda b, h: (b, 0, h))],
        out_specs=pl.BlockSpec((1, seq, hd), lambda b, h: (b, 0, h)),
        out_shape=jax.ShapeDtypeStruct((bsz, seq, seg), jnp.bfloat16),
        compiler_params=pltpu.CompilerParams(
            dimension_semantics=("arbitrary", "arbitrary"),
            vmem_limit_bytes=V7X_VMEM_LIMIT_BYTES,
        ),
        name="moba",
    )(params, q_t, proj4d, v_t, k_means)


def _outproj_kernel(yc_ref, ga_ref, a_ref, x_ref, ag_ref, w_ref, o_ref):
    f32 = jnp.float32
    y_attn = _rms_norm_f32(a_ref[...].astype(f32), ag_ref[...]) * ga_ref[0].astype(f32)
    y = jnp.concatenate([yc_ref[0], y_attn.astype(jnp.bfloat16)], axis=1)
    o_ref[...] = x_ref[...] + jnp.dot(y, w_ref[...], preferred_element_type=f32)


def _outproj(slabs, attn2d, x2d, attn_gain, w_out_bf16):
    _, m, seg = slabs.shape
    d = x2d.shape[1]
    tm = OUT_TILE_M

    def slab_spec(slab):
        return pl.BlockSpec((1, tm, seg), lambda i: (slab, i, 0))

    def whole(shape):
        return pl.BlockSpec(shape, lambda i: (0,) * len(shape))

    return pl.pallas_call(
        _outproj_kernel,
        grid=(m // tm,),
        in_specs=[
            slab_spec(SLAB_CONV), slab_spec(SLAB_GATE),
            pl.BlockSpec((tm, seg), lambda i: (i, 0)),
            pl.BlockSpec((tm, d), lambda i: (i, 0)),
            whole((1, seg)),
            whole(w_out_bf16.shape),
        ],
        out_specs=pl.BlockSpec((tm, d), lambda i: (i, 0)),
        out_shape=jax.ShapeDtypeStruct((m, d), jnp.float32),
        compiler_params=pltpu.CompilerParams(
            dimension_semantics=("arbitrary",),
            vmem_limit_bytes=V7X_VMEM_LIMIT_BYTES,
        ),
        name="outproj",
    )(slabs, slabs, attn2d, x2d, attn_gain.reshape(1, seg), w_out_bf16)


def _head_gains(q_gain, k_gain, seg):
    hd = q_gain.shape[0]
    score_scale = hd ** -0.5 * LOG2E
    rows = jnp.ones((N_SEG, seg), jnp.float32)
    rows = rows.at[SEG_Q].set(jnp.tile(q_gain * score_scale, N_HEADS))
    rows = rows.at[SEG_K].set(jnp.tile(k_gain, N_HEADS))
    return rows.reshape(N_SEG, 1, seg)


def _softmax_shift_params(q_gain, k_gain):
    hd = q_gain.shape[0]
    k_max = math.sqrt(hd) * jnp.max(jnp.abs(k_gain)) * BF16_NORM_SLACK
    q_max = math.sqrt(hd) * jnp.max(jnp.abs(q_gain)) * (hd ** -0.5 * LOG2E) * BF16_NORM_SLACK
    bounded_ok = 2.0 * q_max * k_max <= MAX_SHIFT_GAP
    return jnp.stack([k_max, bounded_ok.astype(jnp.float32)]).astype(jnp.float32)


def kernel(x, norm_gain, w_in, conv_w, q_norm_gain, k_norm_gain, conv_out_gain, attn_out_gain, w_out):
    bsz, seq, d = x.shape
    depth = norm_gain.shape[0]
    seg = w_in.shape[2] // N_SEG
    assert seq % MOBA_BLOCK == 0 and (bsz * seq) % OUT_TILE_M == 0 and (bsz * seq) % IN_TILE_M == 0
    assert w_out.shape[1] == 2 * seg and seg == N_HEADS * q_norm_gain.shape[1]
    x2d = x.reshape(bsz * seq, d)
    for layer in range(depth):
        slabs, q_t, v_t, w_out_bf16, k_means = _inproj(
            x2d, norm_gain[layer], w_in[layer].astype(jnp.bfloat16),
            _head_gains(q_norm_gain[layer], k_norm_gain[layer], seg), w_out[layer],
            conv_w[layer], conv_out_gain[layer], seq)
        attn = _moba(slabs.reshape(N_SLABS, bsz, seq, seg), q_t, v_t,
                     k_means.reshape(bsz, seq // MOBA_BLOCK, seg),
                     _softmax_shift_params(q_norm_gain[layer], k_norm_gain[layer]))
        x2d = _outproj(slabs, attn.reshape(bsz * seq, seg), x2d, attn_out_gain[layer], w_out_bf16)
    return x2d.reshape(bsz, seq, d)
```

```python
import functools
import math

import jax
import jax.numpy as jnp
from jax import lax
from jax.experimental import pallas as pl
from jax.experimental.pallas import tpu as pltpu

N_HEADS = 8
CONV_WIDTH = 3
MOBA_BLOCK = 256
MOBA_TOPK = 3
EPS = 1e-6
N_SEG = 8
N_SLABS = 3
SLAB_CONV, SLAB_K, SLAB_GATE = range(N_SLABS)

MASKED = -1e30
LOG2E = math.log2(math.e)
MAX_SHIFT_GAP = 100.0
BF16_NORM_SLACK = 1.0 + 2.0 ** -6

V7X_VMEM_LIMIT_BYTES = 60 * 1024 * 1024
SUBLANES = 8
LANES = 128

IN_TILE_M = 1024
OUT_TILE_M = 512


def _rms_norm_f32(x, gain):
    return x * lax.rsqrt(jnp.mean(x * x, axis=-1, keepdims=True) + EPS) * gain


def _inproj_kernel(tiles_per_seq, x_ref, g_ref, w_ref, kg_ref, wo_ref, cw_ref, cg_ref, qg_ref,
                   o_ref, qt_ref, vt_ref, wob_ref, km_ref,
                   h_ref, hc_ref, bb_ref, yp_ref, halo_ref):
    i, p = pl.program_id(0), pl.program_id(1)
    f32 = jnp.float32
    hd = qt_ref.shape[1]
    seg = w_ref.shape[1] // 2

    wob_ref[...] = wo_ref[...].astype(wob_ref.dtype)

    def project(which, rows=slice(None)):
        w = w_ref[:, which * seg:(which + 1) * seg]
        return jnp.dot(h_ref[rows], w, preferred_element_type=f32)

    half = h_ref.shape[0] // 2
    row_halves = (slice(0, half), slice(half, 2 * half))

    def head_cols(head):
        return slice(head * hd, (head + 1) * hd)

    def silu(z):
        return z * jax.nn.sigmoid(z)

    @pl.when(p == 0)
    def _():
        for rows in row_halves:
            h = _rms_norm_f32(x_ref[rows], g_ref[...]).astype(h_ref.dtype)
            h_ref[rows] = h
            hc_ref[rows] = jnp.dot(h, w_ref[:, :seg], preferred_element_type=f32).astype(hc_ref.dtype)
        bb_ref[...] = project(1).astype(bb_ref.dtype)

    @pl.when(p == 1)
    def _():
        cw = cw_ref[...]
        prev = jnp.where(i % tiles_per_seq == 0, 0.0, halo_ref[...])
        for rows in row_halves:
            u = project(0, rows) * hc_ref[rows].astype(f32)
            u_ext = jnp.concatenate([prev, u], axis=0)
            u_m1 = pltpu.roll(u_ext, 1, 0)[SUBLANES:]
            u_m2 = pltpu.roll(u_ext, 2, 0)[SUBLANES:]
            conv = u_m2 * cw[0:1] + u_m1 * cw[1:2]
            conv = conv + u * cw[2:3]
            yp_ref[rows] = _rms_norm_f32(bb_ref[rows].astype(f32) * conv, cg_ref[...])
            prev = u[-SUBLANES:]
        halo_ref[...] = prev
        for rows in row_halves:
            o_ref[0, rows] = (yp_ref[rows] * silu(project(1, rows))).astype(o_ref.dtype)

    @pl.when(p == 2)
    def _():
        acc = project(0)
        for head in range(N_HEADS):
            q_t = acc[:, head_cols(head)].T
            rs = lax.rsqrt(jnp.mean(q_t * q_t, axis=0, keepdims=True) + EPS)
            qt_ref[head] = (q_t * rs * qg_ref[...]).astype(qt_ref.dtype)
        for rows in row_halves:
            acc = project(1, rows)
            blocks = slice(rows.start // MOBA_BLOCK, rows.stop // MOBA_BLOCK)
            for head in range(N_HEADS):
                cols = head_cols(head)
                normed = _rms_norm_f32(acc[:, cols], kg_ref[:, cols])
                o_ref[0, rows, cols] = normed.astype(o_ref.dtype)
                km_ref[0, blocks, cols] = jnp.mean(normed.reshape(-1, MOBA_BLOCK, hd), axis=1)

    @pl.when(p == 3)
    def _():
        acc = project(0)
        for head in range(N_HEADS):
            vt_ref[head] = acc[:, head_cols(head)].T.astype(vt_ref.dtype)
        for rows in row_halves:
            o_ref[0, rows] = silu(project(1, rows)).astype(o_ref.dtype)


def _inproj(x2d, gain, w_bf16, q_gain_col, k_gain_row, w_out, conv_w, conv_gain, seq):
    assert IN_TILE_M % MOBA_BLOCK == 0
    m, d = x2d.shape
    seg = w_bf16.shape[1] // N_SEG
    hd = seg // N_HEADS
    tm = IN_TILE_M
    n_tiles = m // tm
    n_steps = N_SEG // 2
    wo_rows = w_out.shape[0] // (n_tiles * n_steps)
    assert wo_rows * n_tiles * n_steps == w_out.shape[0] and wo_rows % (2 * SUBLANES) == 0
    assert seq % tm == 0

    def slab(i, p):
        return (jnp.maximum(p - 1, 0), i, 0)

    def wo_block(i, p):
        return (i * n_steps + p, 0)

    def whole(shape):
        return pl.BlockSpec(shape, lambda i, p: (0,) * len(shape))

    heads_t = pl.BlockSpec((N_HEADS, hd, tm), lambda i, p: (0, 0, i))
    return pl.pallas_call(
        functools.partial(_inproj_kernel, seq // tm),
        grid=(n_tiles, n_steps),
        in_specs=[
            pl.BlockSpec((tm, d), lambda i, p: (i, 0)),
            whole((1, d)),
            pl.BlockSpec((d, 2 * seg), lambda i, p: (0, p)),
            whole((1, seg)),
            pl.BlockSpec((wo_rows, w_out.shape[1]), wo_block),
            whole((CONV_WIDTH, seg)), whole((1, seg)), whole((hd, 1)),
        ],
        out_specs=[
            pl.BlockSpec((1, tm, seg), slab),
            heads_t, heads_t,
            pl.BlockSpec((wo_rows, w_out.shape[1]), wo_block),
            pl.BlockSpec((1, tm // MOBA_BLOCK, seg), lambda i, p: (i, 0, 0)),
        ],
        out_shape=[
            jax.ShapeDtypeStruct((N_SLABS, m, seg), jnp.bfloat16),
            jax.ShapeDtypeStruct((N_HEADS, hd, m), jnp.bfloat16),
            jax.ShapeDtypeStruct((N_HEADS, hd, m), jnp.bfloat16),
            jax.ShapeDtypeStruct(w_out.shape, jnp.bfloat16),
            jax.ShapeDtypeStruct((n_tiles, tm // MOBA_BLOCK, seg), jnp.float32),
        ],
        scratch_shapes=[
            pltpu.VMEM((tm, d), jnp.bfloat16),
            pltpu.VMEM((tm, seg), jnp.bfloat16),
            pltpu.VMEM((tm, seg), jnp.bfloat16),
            pltpu.VMEM((tm, seg), jnp.float32),
            pltpu.VMEM((SUBLANES, seg), jnp.float32),
        ],
        compiler_params=pltpu.CompilerParams(
            dimension_semantics=("arbitrary", "arbitrary"),
            vmem_limit_bytes=V7X_VMEM_LIMIT_BYTES,
        ),
        name="inproj",
    )(x2d, gain.reshape(1, d), w_bf16, k_gain_row, w_out, conv_w, conv_gain.reshape(1, seg),
      q_gain_col)


def _moba_kernel(par_ref, qt_ref, k_ref, vt_ref, km_ref, o_ref):
    f32, bf16 = jnp.float32, jnp.bfloat16
    n_heads, hd, seq = qt_ref.shape
    blk = MOBA_BLOCK
    n_blocks = seq // blk
    assert n_blocks == SUBLANES and hd == LANES

    key_row = lax.broadcasted_iota(jnp.int32, (blk, blk), 0)
    qry_col = lax.broadcasted_iota(jnp.int32, (blk, blk), 1)
    causal = key_row <= qry_col
    blk_row = lax.broadcasted_iota(jnp.int32, (n_blocks, blk), 0)

    def head_cols(head):
        return slice(head * hd, (head + 1) * hd)

    def keys(head, n):
        return k_ref[0, 0, n * blk:(n + 1) * blk, head_cols(head)]

    def block_means(head):
        k_mean = km_ref[0, :, head_cols(head)]
        km_hi = k_mean.astype(bf16)
        km_lo = (k_mean - km_hi.astype(f32)).astype(bf16)
        return jnp.concatenate([km_hi, km_lo], axis=0)

    km_hl = [block_means(head) for head in range(n_heads)]

    def block_bias(head, own, q_t):
        if own <= MOBA_TOPK:
            return None
        g2 = jnp.dot(km_hl[head], q_t, preferred_element_type=f32)
        gate = g2[:n_blocks] + g2[n_blocks:]
        rank = jnp.zeros(gate.shape, f32)
        for m in range(own):
            g_m = gate[m:m + 1, :]
            ahead = (g_m > gate) | ((g_m == gate) & (blk_row > m))
            rank = rank + jnp.where(ahead, 1.0, 0.0)
        return jnp.where((rank < MOBA_TOPK) | (blk_row >= own), 0.0, MASKED)

    def finish(head, own, probs, l):
        n_keys = (own + 1) * blk
        p_all = jnp.concatenate([p.astype(bf16) for p in probs], axis=0)
        acc = jnp.dot(vt_ref[head, :, 0:n_keys], p_all, preferred_element_type=f32)
        out_t = acc / l
        o_ref[0, own * blk:(own + 1) * blk, head_cols(head)] = out_t.T.astype(o_ref.dtype)

    def tile_exact(head, own):
        q_t = qt_ref[head, :, own * blk:(own + 1) * blk]
        bias = block_bias(head, own, q_t)
        parts = []
        for n in range(own + 1):
            part = jnp.dot(keys(head, n), q_t, preferred_element_type=f32)
            if n == own:
                part = jnp.where(causal, part, MASKED)
            elif bias is not None:
                part = part + bias[n:n + 1, :]
            parts.append(part)
        m = functools.reduce(jnp.maximum, [jnp.max(part, axis=0, keepdims=True) for part in parts])
        probs = [jnp.exp2(part - m) for part in parts]
        l = functools.reduce(jnp.add, [jnp.sum(p, axis=0, keepdims=True) for p in probs])
        finish(head, own, probs, l)

    def tile_bounded(head, own):
        q_t = qt_ref[head, :, own * blk:(own + 1) * blk]
        q_f = q_t.astype(f32)
        bound = jnp.sqrt(jnp.sum(q_f * q_f, axis=0, keepdims=True)) * par_ref[0]
        bias = block_bias(head, own, q_t)
        if bias is None:
            bias = jnp.zeros((n_blocks, blk), f32)
        shift = jnp.broadcast_to(-bound, (SUBLANES, blk))
        pad = jnp.zeros((hd - n_blocks - SUBLANES, blk), f32)
        extra = jnp.concatenate([bias, shift, pad], axis=0).astype(bf16)
        rhs = jnp.concatenate([q_t, extra], axis=0)
        lane = lax.broadcasted_iota(jnp.int32, (blk, hd), 1)
        probs = []
        for n in range(own + 1):
            pick = jnp.where((lane == n) | (lane == n_blocks), 1.0, 0.0).astype(bf16)
            lhs = jnp.concatenate([keys(head, n), pick], axis=1)
            part = jnp.dot(lhs, rhs, preferred_element_type=f32)
            if n == own:
                part = jnp.where(causal, part, MASKED)
            probs.append(jnp.exp2(part))
        l = functools.reduce(jnp.add, [jnp.sum(p, axis=0, keepdims=True) for p in probs])
        finish(head, own, probs, l)

    bounded_ok = par_ref[1] > 0.5

    @pl.when(bounded_ok)
    def _():
        for own in range(n_blocks):
            for head in range(n_heads):
                tile_bounded(head, own)

    @pl.when(jnp.logical_not(bounded_ok))
    def _():
        for own in range(n_blocks):
            for head in range(n_heads):
                tile_exact(head, own)


def _moba(proj4d, q_t, v_t, k_means, params):
    _, bsz, seq, seg = proj4d.shape
    hd = seg // N_HEADS
    heads_t = pl.BlockSpec((1, hd, seq), lambda b, h: (h, 0, b))
    return pl.pallas_call(
        _moba_kernel,
        grid=(bsz, N_HEADS),
        in_specs=[pl.BlockSpec(memory_space=pltpu.SMEM), heads_t,
                  pl.BlockSpec((1, 1, seq, hd), lambda b, h: (SLAB_K, b, 0, h)), heads_t,
                  pl.BlockSpec((1, seq // MOBA_BLOCK, hd), lambda b, h: (b, 0, h))],
        out_specs=pl.BlockSpec((1, seq, hd), lambda b, h: (b, 0, h)),
        out_shape=jax.ShapeDtypeStruct((bsz, seq, seg), jnp.bfloat16),
        compiler_params=pltpu.CompilerParams(
            dimension_semantics=("arbitrary", "arbitrary"),
            vmem_limit_bytes=V7X_VMEM_LIMIT_BYTES,
        ),
        name="moba",
    )(params, q_t, proj4d, v_t, k_means)


def _outproj_kernel(yc_ref, ga_ref, a_ref, x_ref, ag_ref, w_ref, o_ref):
    f32 = jnp.float32
    y_attn = _rms_norm_f32(a_ref[...].astype(f32), ag_ref[...]) * ga_ref[0].astype(f32)
    y = jnp.concatenate([yc_ref[0], y_attn.astype(jnp.bfloat16)], axis=1)
    o_ref[...] = x_ref[...] + jnp.dot(y, w_ref[...], preferred_element_type=f32)


def _outproj(slabs, attn2d, x2d, attn_gain, w_out_bf16):
    _, m, seg = slabs.shape
    d = x2d.shape[1]
    tm = OUT_TILE_M

    def slab_spec(slab):
        return pl.BlockSpec((1, tm, seg), lambda i: (slab, i, 0))

    def whole(shape):
        return pl.BlockSpec(shape, lambda i: (0,) * len(shape))

    return pl.pallas_call(
        _outproj_kernel,
        grid=(m // tm,),
        in_specs=[
            slab_spec(SLAB_CONV), slab_spec(SLAB_GATE),
            pl.BlockSpec((tm, seg), lambda i: (i, 0)),
            pl.BlockSpec((tm, d), lambda i: (i, 0)),
            whole((1, seg)),
            whole(w_out_bf16.shape),
        ],
        out_specs=pl.BlockSpec((tm, d), lambda i: (i, 0)),
        out_shape=jax.ShapeDtypeStruct((m, d), jnp.float32),
        compiler_params=pltpu.CompilerParams(
            dimension_semantics=("arbitrary",),
            vmem_limit_bytes=V7X_VMEM_LIMIT_BYTES,
        ),
        name="outproj",
    )(slabs, slabs, attn2d, x2d, attn_gain.reshape(1, seg), w_out_bf16)


def _head_gains(q_gain, k_gain):
    hd = q_gain.shape[0]
    score_scale = hd ** -0.5 * LOG2E
    return (q_gain * score_scale).reshape(hd, 1), jnp.tile(k_gain, N_HEADS).reshape(1, N_HEADS * hd)


def _softmax_shift_params(q_gain, k_gain):
    hd = q_gain.shape[0]
    k_max = math.sqrt(hd) * jnp.max(jnp.abs(k_gain)) * BF16_NORM_SLACK
    q_max = math.sqrt(hd) * jnp.max(jnp.abs(q_gain)) * (hd ** -0.5 * LOG2E) * BF16_NORM_SLACK
    bounded_ok = 2.0 * q_max * k_max <= MAX_SHIFT_GAP
    return jnp.stack([k_max, bounded_ok.astype(jnp.float32)]).astype(jnp.float32)


def kernel(x, norm_gain, w_in, conv_w, q_norm_gain, k_norm_gain, conv_out_gain, attn_out_gain, w_out):
    bsz, seq, d = x.shape
    depth = norm_gain.shape[0]
    seg = w_in.shape[2] // N_SEG
    assert seq % MOBA_BLOCK == 0 and (bsz * seq) % OUT_TILE_M == 0 and (bsz * seq) % IN_TILE_M == 0
    assert w_out.shape[1] == 2 * seg and seg == N_HEADS * q_norm_gain.shape[1]
    x2d = x.reshape(bsz * seq, d)
    for layer in range(depth):
        slabs, q_t, v_t, w_out_bf16, k_means = _inproj(
            x2d, norm_gain[layer], w_in[layer].astype(jnp.bfloat16),
            *_head_gains(q_norm_gain[layer], k_norm_gain[layer]), w_out[layer],
            conv_w[layer], conv_out_gain[layer], seq)
        attn = _moba(slabs.reshape(N_SLABS, bsz, seq, seg), q_t, v_t,
                     k_means.reshape(bsz, seq // MOBA_BLOCK, seg),
                     _softmax_shift_params(q_norm_gain[layer], k_norm_gain[layer]))
        x2d = _outproj(slabs, attn.reshape(bsz * seq, seg), x2d, attn_out_gain[layer], w_out_bf16)
    return x2d.reshape(bsz, seq, d)
```

```python
import functools
import math

import jax
import jax.numpy as jnp
from jax import lax
from jax.experimental import pallas as pl
from jax.experimental.pallas import tpu as pltpu

N_HEADS = 8
CONV_WIDTH = 3
MOBA_BLOCK = 256
MOBA_TOPK = 3
EPS = 1e-6
N_SEG = 8
N_SLABS = 3
SLAB_CONV, SLAB_K, SLAB_GATE = range(N_SLABS)

MASKED = -1e30
LOG2E = math.log2(math.e)
MAX_SHIFT_GAP = 100.0
BF16_NORM_SLACK = 1.0 + 2.0 ** -6

V7X_VMEM_LIMIT_BYTES = 60 * 1024 * 1024
SUBLANES = 8
LANES = 128

IN_TILE_M = 1024
OUT_TILE_M = 512
MOBA_HEADS_PER_STEP = 4


def _rms_norm_f32(x, gain):
    return x * lax.rsqrt(jnp.mean(x * x, axis=-1, keepdims=True) + EPS) * gain


def _inproj_kernel(tiles_per_seq, x_ref, g_ref, w_ref, kg_ref, wo_ref, cw_ref, cg_ref, qg_ref,
                   o_ref, qt_ref, vt_ref, wob_ref, km_ref,
                   h_ref, hc_ref, bb_ref, yp_ref, halo_ref):
    i, p = pl.program_id(0), pl.program_id(1)
    f32 = jnp.float32
    hd = qt_ref.shape[1]
    seg = w_ref.shape[1] // 2

    wob_ref[...] = wo_ref[...].astype(wob_ref.dtype)

    def project(which, rows=slice(None)):
        w = w_ref[:, which * seg:(which + 1) * seg]
        return jnp.dot(h_ref[rows], w, preferred_element_type=f32)

    half = h_ref.shape[0] // 2
    row_halves = (slice(0, half), slice(half, 2 * half))

    def head_cols(head):
        return slice(head * hd, (head + 1) * hd)

    def silu(z):
        return z * jax.nn.sigmoid(z)

    @pl.when(p == 0)
    def _():
        for rows in row_halves:
            h = _rms_norm_f32(x_ref[rows], g_ref[...]).astype(h_ref.dtype)
            h_ref[rows] = h
            hc_ref[rows] = jnp.dot(h, w_ref[:, :seg], preferred_element_type=f32).astype(hc_ref.dtype)
        bb_ref[...] = project(1).astype(bb_ref.dtype)

    @pl.when(p == 1)
    def _():
        cw = cw_ref[...]
        prev = jnp.where(i % tiles_per_seq == 0, 0.0, halo_ref[...])
        for rows in row_halves:
            u = project(0, rows) * hc_ref[rows].astype(f32)
            u_ext = jnp.concatenate([prev, u], axis=0)
            u_m1 = pltpu.roll(u_ext, 1, 0)[SUBLANES:]
            u_m2 = pltpu.roll(u_ext, 2, 0)[SUBLANES:]
            conv = u_m2 * cw[0:1] + u_m1 * cw[1:2]
            conv = conv + u * cw[2:3]
            yp_ref[rows] = _rms_norm_f32(bb_ref[rows].astype(f32) * conv, cg_ref[...])
            prev = u[-SUBLANES:]
        halo_ref[...] = prev
        for rows in row_halves:
            o_ref[0, rows] = (yp_ref[rows] * silu(project(1, rows))).astype(o_ref.dtype)

    @pl.when(p == 2)
    def _():
        acc = project(0)
        for head in range(N_HEADS):
            q_t = acc[:, head_cols(head)].T
            rs = lax.rsqrt(jnp.mean(q_t * q_t, axis=0, keepdims=True) + EPS)
            qt_ref[head] = (q_t * rs * qg_ref[...]).astype(qt_ref.dtype)
        for rows in row_halves:
            acc = project(1, rows)
            blocks = slice(rows.start // MOBA_BLOCK, rows.stop // MOBA_BLOCK)
            for head in range(N_HEADS):
                cols = head_cols(head)
                normed = _rms_norm_f32(acc[:, cols], kg_ref[:, cols])
                o_ref[0, rows, cols] = normed.astype(o_ref.dtype)
                km_ref[0, blocks, cols] = jnp.mean(normed.reshape(-1, MOBA_BLOCK, hd), axis=1)

    @pl.when(p == 3)
    def _():
        acc = project(0)
        for head in range(N_HEADS):
            vt_ref[head] = acc[:, head_cols(head)].T.astype(vt_ref.dtype)
        for rows in row_halves:
            o_ref[0, rows] = silu(project(1, rows)).astype(o_ref.dtype)


def _inproj(x2d, gain, w_bf16, q_gain_col, k_gain_row, w_out, conv_w, conv_gain, seq):
    assert IN_TILE_M % MOBA_BLOCK == 0
    m, d = x2d.shape
    seg = w_bf16.shape[1] // N_SEG
    hd = seg // N_HEADS
    tm = IN_TILE_M
    n_tiles = m // tm
    n_steps = N_SEG // 2
    wo_rows = w_out.shape[0] // (n_tiles * n_steps)
    assert wo_rows * n_tiles * n_steps == w_out.shape[0] and wo_rows % (2 * SUBLANES) == 0
    assert seq % tm == 0

    def slab(i, p):
        return (jnp.maximum(p - 1, 0), i, 0)

    def wo_block(i, p):
        return (i * n_steps + p, 0)

    def whole(shape):
        return pl.BlockSpec(shape, lambda i, p: (0,) * len(shape))

    heads_t = pl.BlockSpec((N_HEADS, hd, tm), lambda i, p: (0, 0, i))
    return pl.pallas_call(
        functools.partial(_inproj_kernel, seq // tm),
        grid=(n_tiles, n_steps),
        in_specs=[
            pl.BlockSpec((tm, d), lambda i, p: (i, 0)),
            whole((1, d)),
            pl.BlockSpec((d, 2 * seg), lambda i, p: (0, p)),
            whole((1, seg)),
            pl.BlockSpec((wo_rows, w_out.shape[1]), wo_block),
            whole((CONV_WIDTH, seg)), whole((1, seg)), whole((hd, 1)),
        ],
        out_specs=[
            pl.BlockSpec((1, tm, seg), slab),
            heads_t, heads_t,
            pl.BlockSpec((wo_rows, w_out.shape[1]), wo_block),
            pl.BlockSpec((1, tm // MOBA_BLOCK, seg), lambda i, p: (i, 0, 0)),
        ],
        out_shape=[
            jax.ShapeDtypeStruct((N_SLABS, m, seg), jnp.bfloat16),
            jax.ShapeDtypeStruct((N_HEADS, hd, m), jnp.bfloat16),
            jax.ShapeDtypeStruct((N_HEADS, hd, m), jnp.bfloat16),
            jax.ShapeDtypeStruct(w_out.shape, jnp.bfloat16),
            jax.ShapeDtypeStruct((n_tiles, tm // MOBA_BLOCK, seg), jnp.float32),
        ],
        scratch_shapes=[
            pltpu.VMEM((tm, d), jnp.bfloat16),
            pltpu.VMEM((tm, seg), jnp.bfloat16),
            pltpu.VMEM((tm, seg), jnp.bfloat16),
            pltpu.VMEM((tm, seg), jnp.float32),
            pltpu.VMEM((SUBLANES, seg), jnp.float32),
        ],
        compiler_params=pltpu.CompilerParams(
            dimension_semantics=("arbitrary", "arbitrary"),
            vmem_limit_bytes=V7X_VMEM_LIMIT_BYTES,
        ),
        name="inproj",
    )(x2d, gain.reshape(1, d), w_bf16, k_gain_row, w_out, conv_w, conv_gain.reshape(1, seg),
      q_gain_col)


def _moba_kernel(par_ref, qt_ref, k_ref, vt_ref, km_ref, o_ref):
    f32, bf16 = jnp.float32, jnp.bfloat16
    n_heads, hd, seq = qt_ref.shape
    blk = MOBA_BLOCK
    n_blocks = seq // blk
    assert n_blocks == SUBLANES and hd == LANES

    key_row = lax.broadcasted_iota(jnp.int32, (blk, blk), 0)
    qry_col = lax.broadcasted_iota(jnp.int32, (blk, blk), 1)
    causal = key_row <= qry_col
    blk_row = lax.broadcasted_iota(jnp.int32, (n_blocks, blk), 0)

    def head_cols(head):
        return slice(head * hd, (head + 1) * hd)

    def keys(head, n):
        return k_ref[0, 0, n * blk:(n + 1) * blk, head_cols(head)]

    def block_means(head):
        k_mean = km_ref[0, :, head_cols(head)]
        km_hi = k_mean.astype(bf16)
        km_lo = (k_mean - km_hi.astype(f32)).astype(bf16)
        return jnp.concatenate([km_hi, km_lo], axis=0)

    km_hl = [block_means(head) for head in range(n_heads)]

    def block_bias(head, own, q_t):
        if own <= MOBA_TOPK:
            return None
        g2 = jnp.dot(km_hl[head], q_t, preferred_element_type=f32)
        gate = g2[:n_blocks] + g2[n_blocks:]
        rank = jnp.zeros(gate.shape, f32)
        for m in range(own):
            g_m = gate[m:m + 1, :]
            ahead = (g_m > gate) | ((g_m == gate) & (blk_row > m))
            rank = rank + jnp.where(ahead, 1.0, 0.0)
        return jnp.where((rank < MOBA_TOPK) | (blk_row >= own), 0.0, MASKED)

    def finish(head, own, probs, l):
        n_keys = (own + 1) * blk
        p_all = jnp.concatenate([p.astype(bf16) for p in probs], axis=0)
        acc = jnp.dot(vt_ref[head, :, 0:n_keys], p_all, preferred_element_type=f32)
        out_t = acc / l
        o_ref[0, own * blk:(own + 1) * blk, head_cols(head)] = out_t.T.astype(o_ref.dtype)

    def tile_exact(head, own):
        q_t = qt_ref[head, :, own * blk:(own + 1) * blk]
        bias = block_bias(head, own, q_t)
        parts = []
        for n in range(own + 1):
            part = jnp.dot(keys(head, n), q_t, preferred_element_type=f32)
            if n == own:
                part = jnp.where(causal, part, MASKED)
            elif bias is not None:
                part = part + bias[n:n + 1, :]
            parts.append(part)
        m = functools.reduce(jnp.maximum, [jnp.max(part, axis=0, keepdims=True) for part in parts])
        probs = [jnp.exp2(part - m) for part in parts]
        l = functools.reduce(jnp.add, [jnp.sum(p, axis=0, keepdims=True) for p in probs])
        finish(head, own, probs, l)

    def tile_bounded(head, own):
        q_t = qt_ref[head, :, own * blk:(own + 1) * blk]
        q_f = q_t.astype(f32)
        bound = jnp.sqrt(jnp.sum(q_f * q_f, axis=0, keepdims=True)) * par_ref[0]
        bias = block_bias(head, own, q_t)
        if bias is None:
            bias = jnp.zeros((n_blocks, blk), f32)
        shift = jnp.broadcast_to(-bound, (SUBLANES, blk))
        pad = jnp.zeros((hd - n_blocks - SUBLANES, blk), f32)
        extra = jnp.concatenate([bias, shift, pad], axis=0).astype(bf16)
        rhs = jnp.concatenate([q_t, extra], axis=0)
        lane = lax.broadcasted_iota(jnp.int32, (blk, hd), 1)
        probs = []
        for n in range(own + 1):
            pick = jnp.where((lane == n) | (lane == n_blocks), 1.0, 0.0).astype(bf16)
            lhs = jnp.concatenate([keys(head, n), pick], axis=1)
            part = jnp.dot(lhs, rhs, preferred_element_type=f32)
            if n == own:
                part = jnp.where(causal, part, MASKED)
            probs.append(jnp.exp2(part))
        l = functools.reduce(jnp.add, [jnp.sum(p, axis=0, keepdims=True) for p in probs])
        finish(head, own, probs, l)

    bounded_ok = par_ref[1] > 0.5

    @pl.when(bounded_ok)
    def _():
        for own in range(n_blocks):
            for head in range(n_heads):
                tile_bounded(head, own)

    @pl.when(jnp.logical_not(bounded_ok))
    def _():
        for own in range(n_blocks):
            for head in range(n_heads):
                tile_exact(head, own)


def _moba(proj4d, q_t, v_t, k_means, params):
    _, bsz, seq, seg = proj4d.shape
    hd = seg // N_HEADS
    grp = MOBA_HEADS_PER_STEP
    heads_t = pl.BlockSpec((grp, hd, seq), lambda b, g: (g, 0, b))
    return pl.pallas_call(
        _moba_kernel,
        grid=(bsz, N_HEADS // grp),
        in_specs=[pl.BlockSpec(memory_space=pltpu.SMEM), heads_t,
                  pl.BlockSpec((1, 1, seq, grp * hd), lambda b, g: (SLAB_K, b, 0, g)), heads_t,
                  pl.BlockSpec((1, seq // MOBA_BLOCK, grp * hd), lambda b, g: (b, 0, g))],
        out_specs=pl.BlockSpec((1, seq, grp * hd), lambda b, g: (b, 0, g)),
        out_shape=jax.ShapeDtypeStruct((bsz, seq, seg), jnp.bfloat16),
        compiler_params=pltpu.CompilerParams(
            dimension_semantics=("arbitrary", "arbitrary"),
            vmem_limit_bytes=V7X_VMEM_LIMIT_BYTES,
        ),
        name="moba",
    )(params, q_t, proj4d, v_t, k_means)


def _outproj_kernel(yc_ref, ga_ref, a_ref, x_ref, ag_ref, w_ref, o_ref):
    f32 = jnp.float32
    y_attn = _rms_norm_f32(a_ref[...].astype(f32), ag_ref[...]) * ga_ref[0].astype(f32)
    y = jnp.concatenate([yc_ref[0], y_attn.astype(jnp.bfloat16)], axis=1)
    o_ref[...] = x_ref[...] + jnp.dot(y, w_ref[...], preferred_element_type=f32)


def _outproj(slabs, attn2d, x2d, attn_gain, w_out_bf16):
    _, m, seg = slabs.shape
    d = x2d.shape[1]
    tm = OUT_TILE_M

    def slab_spec(slab):
        return pl.BlockSpec((1, tm, seg), lambda i: (slab, i, 0))

    def whole(shape):
        return pl.BlockSpec(shape, lambda i: (0,) * len(shape))

    return pl.pallas_call(
        _outproj_kernel,
        grid=(m // tm,),
        in_specs=[
            slab_spec(SLAB_CONV), slab_spec(SLAB_GATE),
            pl.BlockSpec((tm, seg), lambda i: (i, 0)),
            pl.BlockSpec((tm, d), lambda i: (i, 0)),
            whole((1, seg)),
            whole(w_out_bf16.shape),
        ],
        out_specs=pl.BlockSpec((tm, d), lambda i: (i, 0)),
        out_shape=jax.ShapeDtypeStruct((m, d), jnp.float32),
        compiler_params=pltpu.CompilerParams(
            dimension_semantics=("arbitrary",),
            vmem_limit_bytes=V7X_VMEM_LIMIT_BYTES,
        ),
        name="outproj",
    )(slabs, slabs, attn2d, x2d, attn_gain.reshape(1, seg), w_out_bf16)


def _head_gains(q_gain, k_gain):
    hd = q_gain.shape[0]
    score_scale = hd ** -0.5 * LOG2E
    return (q_gain * score_scale).reshape(hd, 1), jnp.tile(k_gain, N_HEADS).reshape(1, N_HEADS * hd)


def _softmax_shift_params(q_gain, k_gain):
    hd = q_gain.shape[0]
    k_max = math.sqrt(hd) * jnp.max(jnp.abs(k_gain)) * BF16_NORM_SLACK
    q_max = math.sqrt(hd) * jnp.max(jnp.abs(q_gain)) * (hd ** -0.5 * LOG2E) * BF16_NORM_SLACK
    bounded_ok = 2.0 * q_max * k_max <= MAX_SHIFT_GAP
    return jnp.stack([k_max, bounded_ok.astype(jnp.float32)]).astype(jnp.float32)


def kernel(x, norm_gain, w_in, conv_w, q_norm_gain, k_norm_gain, conv_out_gain, attn_out_gain, w_out):
    bsz, seq, d = x.shape
    depth = norm_gain.shape[0]
    seg = w_in.shape[2] // N_SEG
    assert seq % MOBA_BLOCK == 0 and (bsz * seq) % OUT_TILE_M == 0 and (bsz * seq) % IN_TILE_M == 0
    assert w_out.shape[1] == 2 * seg and seg == N_HEADS * q_norm_gain.shape[1]
    x2d = x.reshape(bsz * seq, d)
    for layer in range(depth):
        slabs, q_t, v_t, w_out_bf16, k_means = _inproj(
            x2d, norm_gain[layer], w_in[layer].astype(jnp.bfloat16),
            *_head_gains(q_norm_gain[layer], k_norm_gain[layer]), w_out[layer],
            conv_w[layer], conv_out_gain[layer], seq)
        attn = _moba(slabs.reshape(N_SLABS, bsz, seq, seg), q_t, v_t,
                     k_means.reshape(bsz, seq // MOBA_BLOCK, seg),
                     _softmax_shift_params(q_norm_gain[layer], k_norm_gain[layer]))
        x2d = _outproj(slabs, attn.reshape(bsz * seq, seg), x2d, attn_out_gain[layer], w_out_bf16)
    return x2d.reshape(bsz, seq, d)
```

```python
import functools
import math

import jax
import jax.numpy as jnp
from jax import lax
from jax.experimental import pallas as pl
from jax.experimental.pallas import tpu as pltpu

N_HEADS = 8
CONV_WIDTH = 3
MOBA_BLOCK = 256
MOBA_TOPK = 3
EPS = 1e-6
N_SEG = 8
N_SLABS = 3
SLAB_CONV, SLAB_K, SLAB_GATE = range(N_SLABS)

MASKED = -1e30
LOG2E = math.log2(math.e)
MAX_SHIFT_GAP = 100.0
BF16_NORM_SLACK = 1.0 + 2.0 ** -6

V7X_VMEM_LIMIT_BYTES = 60 * 1024 * 1024
SUBLANES = 8
LANES = 128

IN_TILE_M = 1024
OUT_TILE_M = 1024


def _rms_norm_f32(x, gain):
    return x * lax.rsqrt(jnp.mean(x * x, axis=-1, keepdims=True) + EPS) * gain


def _inproj_kernel(tiles_per_seq, x_ref, g_ref, w_ref, kg_ref, wo_ref, cw_ref, cg_ref, qg_ref,
                   o_ref, qt_ref, vt_ref, wob_ref, km_ref,
                   h_ref, hc_ref, bb_ref, yp_ref, halo_ref):
    i, p = pl.program_id(0), pl.program_id(1)
    f32 = jnp.float32
    hd = qt_ref.shape[1]
    seg = w_ref.shape[1] // 2

    wob_ref[...] = wo_ref[...].astype(wob_ref.dtype)

    def project(which, rows=slice(None)):
        w = w_ref[:, which * seg:(which + 1) * seg]
        return jnp.dot(h_ref[rows], w, preferred_element_type=f32)

    half = h_ref.shape[0] // 2
    row_halves = (slice(0, half), slice(half, 2 * half))

    def head_cols(head):
        return slice(head * hd, (head + 1) * hd)

    def silu(z):
        return z * jax.nn.sigmoid(z)

    @pl.when(p == 0)
    def _():
        for rows in row_halves:
            h = _rms_norm_f32(x_ref[rows], g_ref[...]).astype(h_ref.dtype)
            h_ref[rows] = h
            hc_ref[rows] = jnp.dot(h, w_ref[:, :seg], preferred_element_type=f32).astype(hc_ref.dtype)
        bb_ref[...] = project(1).astype(bb_ref.dtype)

    @pl.when(p == 1)
    def _():
        cw = cw_ref[...]
        prev = jnp.where(i % tiles_per_seq == 0, 0.0, halo_ref[...])
        for rows in row_halves:
            u = project(0, rows) * hc_ref[rows].astype(f32)
            u_ext = jnp.concatenate([prev, u], axis=0)
            u_m1 = pltpu.roll(u_ext, 1, 0)[SUBLANES:]
            u_m2 = pltpu.roll(u_ext, 2, 0)[SUBLANES:]
            conv = u_m2 * cw[0:1] + u_m1 * cw[1:2]
            conv = conv + u * cw[2:3]
            yp_ref[rows] = _rms_norm_f32(bb_ref[rows].astype(f32) * conv, cg_ref[...])
            prev = u[-SUBLANES:]
        halo_ref[...] = prev
        for rows in row_halves:
            o_ref[0, rows] = (yp_ref[rows] * silu(project(1, rows))).astype(o_ref.dtype)

    @pl.when(p == 2)
    def _():
        acc = project(0)
        for head in range(N_HEADS):
            q_t = acc[:, head_cols(head)].T
            rs = lax.rsqrt(jnp.mean(q_t * q_t, axis=0, keepdims=True) + EPS)
            qt_ref[head] = (q_t * rs * qg_ref[...]).astype(qt_ref.dtype)
        for rows in row_halves:
            acc = project(1, rows)
            blocks = slice(rows.start // MOBA_BLOCK, rows.stop // MOBA_BLOCK)
            for head in range(N_HEADS):
                cols = head_cols(head)
                normed = _rms_norm_f32(acc[:, cols], kg_ref[:, cols])
                o_ref[0, rows, cols] = normed.astype(o_ref.dtype)
                km_ref[0, blocks, cols] = jnp.mean(normed.reshape(-1, MOBA_BLOCK, hd), axis=1)

    @pl.when(p == 3)
    def _():
        acc = project(0)
        for head in range(N_HEADS):
            vt_ref[head] = acc[:, head_cols(head)].T.astype(vt_ref.dtype)
        for rows in row_halves:
            o_ref[0, rows] = silu(project(1, rows)).astype(o_ref.dtype)


def _inproj(x2d, gain, w_bf16, q_gain_col, k_gain_row, w_out, conv_w, conv_gain, seq):
    assert IN_TILE_M % MOBA_BLOCK == 0
    m, d = x2d.shape
    seg = w_bf16.shape[1] // N_SEG
    hd = seg // N_HEADS
    tm = IN_TILE_M
    n_tiles = m // tm
    n_steps = N_SEG // 2
    wo_rows = w_out.shape[0] // (n_tiles * n_steps)
    assert wo_rows * n_tiles * n_steps == w_out.shape[0] and wo_rows % (2 * SUBLANES) == 0
    assert seq % tm == 0

    def slab(i, p):
        return (jnp.maximum(p - 1, 0), i, 0)

    def wo_block(i, p):
        return (i * n_steps + p, 0)

    def whole(shape):
        return pl.BlockSpec(shape, lambda i, p: (0,) * len(shape))

    heads_t = pl.BlockSpec((N_HEADS, hd, tm), lambda i, p: (0, 0, i))
    return pl.pallas_call(
        functools.partial(_inproj_kernel, seq // tm),
        grid=(n_tiles, n_steps),
        in_specs=[
            pl.BlockSpec((tm, d), lambda i, p: (i, 0)),
            whole((1, d)),
            pl.BlockSpec((d, 2 * seg), lambda i, p: (0, p)),
            whole((1, seg)),
            pl.BlockSpec((wo_rows, w_out.shape[1]), wo_block),
            whole((CONV_WIDTH, seg)), whole((1, seg)), whole((hd, 1)),
        ],
        out_specs=[
            pl.BlockSpec((1, tm, seg), slab),
            heads_t, heads_t,
            pl.BlockSpec((wo_rows, w_out.shape[1]), wo_block),
            pl.BlockSpec((1, tm // MOBA_BLOCK, seg), lambda i, p: (i, 0, 0)),
        ],
        out_shape=[
            jax.ShapeDtypeStruct((N_SLABS, m, seg), jnp.bfloat16),
            jax.ShapeDtypeStruct((N_HEADS, hd, m), jnp.bfloat16),
            jax.ShapeDtypeStruct((N_HEADS, hd, m), jnp.bfloat16),
            jax.ShapeDtypeStruct(w_out.shape, jnp.bfloat16),
            jax.ShapeDtypeStruct((n_tiles, tm // MOBA_BLOCK, seg), jnp.float32),
        ],
        scratch_shapes=[
            pltpu.VMEM((tm, d), jnp.bfloat16),
            pltpu.VMEM((tm, seg), jnp.bfloat16),
            pltpu.VMEM((tm, seg), jnp.bfloat16),
            pltpu.VMEM((tm, seg), jnp.float32),
            pltpu.VMEM((SUBLANES, seg), jnp.float32),
        ],
        compiler_params=pltpu.CompilerParams(
            dimension_semantics=("arbitrary", "arbitrary"),
            vmem_limit_bytes=V7X_VMEM_LIMIT_BYTES,
        ),
        name="inproj",
    )(x2d, gain.reshape(1, d), w_bf16, k_gain_row, w_out, conv_w, conv_gain.reshape(1, seg),
      q_gain_col)


def _moba_kernel(par_ref, qt_ref, k_ref, vt_ref, km_ref, o_ref):
    f32, bf16 = jnp.float32, jnp.bfloat16
    n_heads, hd, seq = qt_ref.shape
    blk = MOBA_BLOCK
    n_blocks = seq // blk
    assert n_blocks == SUBLANES and hd == LANES

    key_row = lax.broadcasted_iota(jnp.int32, (blk, blk), 0)
    qry_col = lax.broadcasted_iota(jnp.int32, (blk, blk), 1)
    causal = key_row <= qry_col
    blk_row = lax.broadcasted_iota(jnp.int32, (n_blocks, blk), 0)

    def head_cols(head):
        return slice(head * hd, (head + 1) * hd)

    def keys(head, n):
        return k_ref[0, 0, n * blk:(n + 1) * blk, head_cols(head)]

    def block_means(head):
        k_mean = km_ref[0, :, head_cols(head)]
        km_hi = k_mean.astype(bf16)
        km_lo = (k_mean - km_hi.astype(f32)).astype(bf16)
        return jnp.concatenate([km_hi, km_lo], axis=0)

    km_hl = [block_means(head) for head in range(n_heads)]

    def block_bias(head, own, q_t):
        if own <= MOBA_TOPK:
            return None
        g2 = jnp.dot(km_hl[head], q_t, preferred_element_type=f32)
        gate = g2[:n_blocks] + g2[n_blocks:]
        rank = jnp.zeros(gate.shape, f32)
        for m in range(own):
            g_m = gate[m:m + 1, :]
            ahead = (g_m > gate) | ((g_m == gate) & (blk_row > m))
            rank = rank + jnp.where(ahead, 1.0, 0.0)
        return jnp.where((rank < MOBA_TOPK) | (blk_row >= own), 0.0, MASKED)

    def finish(head, own, probs, l):
        n_keys = (own + 1) * blk
        p_all = jnp.concatenate([p.astype(bf16) for p in probs], axis=0)
        acc = jnp.dot(vt_ref[head, :, 0:n_keys], p_all, preferred_element_type=f32)
        out_t = acc / l
        o_ref[0, own * blk:(own + 1) * blk, head_cols(head)] = out_t.T.astype(o_ref.dtype)

    def tile_exact(head, own):
        q_t = qt_ref[head, :, own * blk:(own + 1) * blk]
        bias = block_bias(head, own, q_t)
        parts = []
        for n in range(own + 1):
            part = jnp.dot(keys(head, n), q_t, preferred_element_type=f32)
            if n == own:
                part = jnp.where(causal, part, MASKED)
            elif bias is not None:
                part = part + bias[n:n + 1, :]
            parts.append(part)
        m = functools.reduce(jnp.maximum, [jnp.max(part, axis=0, keepdims=True) for part in parts])
        probs = [jnp.exp2(part - m) for part in parts]
        l = functools.reduce(jnp.add, [jnp.sum(p, axis=0, keepdims=True) for p in probs])
        finish(head, own, probs, l)

    def tile_bounded(head, own):
        q_t = qt_ref[head, :, own * blk:(own + 1) * blk]
        q_f = q_t.astype(f32)
        bound = jnp.sqrt(jnp.sum(q_f * q_f, axis=0, keepdims=True)) * par_ref[0]
        bias = block_bias(head, own, q_t)
        if bias is None:
            bias = jnp.zeros((n_blocks, blk), f32)
        shift = jnp.broadcast_to(-bound, (SUBLANES, blk))
        pad = jnp.zeros((hd - n_blocks - SUBLANES, blk), f32)
        extra = jnp.concatenate([bias, shift, pad], axis=0).astype(bf16)
        rhs = jnp.concatenate([q_t, extra], axis=0)
        lane = lax.broadcasted_iota(jnp.int32, (blk, hd), 1)
        probs = []
        for n in range(own + 1):
            pick = jnp.where((lane == n) | (lane == n_blocks), 1.0, 0.0).astype(bf16)
            lhs = jnp.concatenate([keys(head, n), pick], axis=1)
            part = jnp.dot(lhs, rhs, preferred_element_type=f32)
            if n == own:
                part = jnp.where(causal, part, MASKED)
            probs.append(jnp.exp2(part))
        l = functools.reduce(jnp.add, [jnp.sum(p, axis=0, keepdims=True) for p in probs])
        finish(head, own, probs, l)

    bounded_ok = par_ref[1] > 0.5

    @pl.when(bounded_ok)
    def _():
        for own in range(n_blocks):
            for head in range(n_heads):
                tile_bounded(head, own)

    @pl.when(jnp.logical_not(bounded_ok))
    def _():
        for own in range(n_blocks):
            for head in range(n_heads):
                tile_exact(head, own)


def _moba(proj4d, q_t, v_t, k_means, params):
    _, bsz, seq, seg = proj4d.shape
    hd = seg // N_HEADS
    heads_t = pl.BlockSpec((1, hd, seq), lambda b, h: (h, 0, b))
    return pl.pallas_call(
        _moba_kernel,
        grid=(bsz, N_HEADS),
        in_specs=[pl.BlockSpec(memory_space=pltpu.SMEM), heads_t,
                  pl.BlockSpec((1, 1, seq, hd), lambda b, h: (SLAB_K, b, 0, h)), heads_t,
                  pl.BlockSpec((1, seq // MOBA_BLOCK, hd), lambda b, h: (b, 0, h))],
        out_specs=pl.BlockSpec((1, seq, hd), lambda b, h: (b, 0, h)),
        out_shape=jax.ShapeDtypeStruct((bsz, seq, seg), jnp.bfloat16),
        compiler_params=pltpu.CompilerParams(
            dimension_semantics=("arbitrary", "arbitrary"),
            vmem_limit_bytes=V7X_VMEM_LIMIT_BYTES,
        ),
        name="moba",
    )(params, q_t, proj4d, v_t, k_means)


def _outproj_kernel(yc_ref, ga_ref, a_ref, x_ref, ag_ref, w_ref, o_ref):
    f32 = jnp.float32
    half = o_ref.shape[0] // 2
    for rows in (slice(0, half), slice(half, 2 * half)):
        y_attn = _rms_norm_f32(a_ref[rows].astype(f32), ag_ref[...]) * ga_ref[0, rows].astype(f32)
        y = jnp.concatenate([yc_ref[0, rows], y_attn.astype(jnp.bfloat16)], axis=1)
        o_ref[rows] = x_ref[rows] + jnp.dot(y, w_ref[...], preferred_element_type=f32)


def _outproj(slabs, attn2d, x2d, attn_gain, w_out_bf16):
    _, m, seg = slabs.shape
    d = x2d.shape[1]
    tm = OUT_TILE_M

    def slab_spec(slab):
        return pl.BlockSpec((1, tm, seg), lambda i: (slab, i, 0))

    def whole(shape):
        return pl.BlockSpec(shape, lambda i: (0,) * len(shape))

    return pl.pallas_call(
        _outproj_kernel,
        grid=(m // tm,),
        in_specs=[
            slab_spec(SLAB_CONV), slab_spec(SLAB_GATE),
            pl.BlockSpec((tm, seg), lambda i: (i, 0)),
            pl.BlockSpec((tm, d), lambda i: (i, 0)),
            whole((1, seg)),
            pl.BlockSpec(w_out_bf16.shape, lambda i: (0, 0), pipeline_mode=pl.Buffered(1)),
        ],
        out_specs=pl.BlockSpec((tm, d), lambda i: (i, 0)),
        out_shape=jax.ShapeDtypeStruct((m, d), jnp.float32),
        compiler_params=pltpu.CompilerParams(
            dimension_semantics=("arbitrary",),
            vmem_limit_bytes=V7X_VMEM_LIMIT_BYTES,
        ),
        name="outproj",
    )(slabs, slabs, attn2d, x2d, attn_gain.reshape(1, seg), w_out_bf16)


def _head_gains(q_gain, k_gain):
    hd = q_gain.shape[0]
    score_scale = hd ** -0.5 * LOG2E
    return (q_gain * score_scale).reshape(hd, 1), jnp.tile(k_gain, N_HEADS).reshape(1, N_HEADS * hd)


def _softmax_shift_params(q_gain, k_gain):
    hd = q_gain.shape[0]
    k_max = math.sqrt(hd) * jnp.max(jnp.abs(k_gain)) * BF16_NORM_SLACK
    q_max = math.sqrt(hd) * jnp.max(jnp.abs(q_gain)) * (hd ** -0.5 * LOG2E) * BF16_NORM_SLACK
    bounded_ok = 2.0 * q_max * k_max <= MAX_SHIFT_GAP
    return jnp.stack([k_max, bounded_ok.astype(jnp.float32)]).astype(jnp.float32)


def kernel(x, norm_gain, w_in, conv_w, q_norm_gain, k_norm_gain, conv_out_gain, attn_out_gain, w_out):
    bsz, seq, d = x.shape
    depth = norm_gain.shape[0]
    seg = w_in.shape[2] // N_SEG
    assert seq % MOBA_BLOCK == 0 and (bsz * seq) % OUT_TILE_M == 0 and (bsz * seq) % IN_TILE_M == 0
    assert w_out.shape[1] == 2 * seg and seg == N_HEADS * q_norm_gain.shape[1]
    x2d = x.reshape(bsz * seq, d)
    for layer in range(depth):
        slabs, q_t, v_t, w_out_bf16, k_means = _inproj(
            x2d, norm_gain[layer], w_in[layer].astype(jnp.bfloat16),
            *_head_gains(q_norm_gain[layer], k_norm_gain[layer]), w_out[layer],
            conv_w[layer], conv_out_gain[layer], seq)
        attn = _moba(slabs.reshape(N_SLABS, bsz, seq, seg), q_t, v_t,
                     k_means.reshape(bsz, seq // MOBA_BLOCK, seg),
                     _softmax_shift_params(q_norm_gain[layer], k_norm_gain[layer]))
        x2d = _outproj(slabs, attn.reshape(bsz * seq, seg), x2d, attn_out_gain[layer], w_out_bf16)
    return x2d.reshape(bsz, seq, d)
```

```python
import functools
import math

import jax
import jax.numpy as jnp
from jax import lax
from jax.experimental import pallas as pl
from jax.experimental.pallas import tpu as pltpu

N_HEADS = 8
CONV_WIDTH = 3
MOBA_BLOCK = 256
MOBA_TOPK = 3
EPS = 1e-6
N_SEG = 8
N_SLABS = 3
SLAB_CONV, SLAB_K, SLAB_GATE = range(N_SLABS)

MASKED = -1e30
LOG2E = math.log2(math.e)
MAX_SHIFT_GAP = 100.0
BF16_NORM_SLACK = 1.0 + 2.0 ** -6

V7X_VMEM_LIMIT_BYTES = 60 * 1024 * 1024
SUBLANES = 8
LANES = 128

IN_TILE_M = 1024
OUT_TILE_M = 1024
MOBA_HEADS_PER_STEP = 4


def _rms_norm_f32(x, gain):
    return x * lax.rsqrt(jnp.mean(x * x, axis=-1, keepdims=True) + EPS) * gain


def _inproj_kernel(tiles_per_seq, x_ref, g_ref, w_ref, kg_ref, wo_ref, cw_ref, cg_ref, qg_ref,
                   o_ref, qt_ref, vt_ref, wob_ref, km_ref,
                   h_ref, hc_ref, bb_ref, yp_ref, halo_ref):
    i, p = pl.program_id(0), pl.program_id(1)
    f32 = jnp.float32
    hd = qt_ref.shape[1]
    seg = w_ref.shape[1] // 2

    wob_ref[...] = wo_ref[...].astype(wob_ref.dtype)

    def project(which, rows=slice(None)):
        w = w_ref[:, which * seg:(which + 1) * seg]
        return jnp.dot(h_ref[rows], w, preferred_element_type=f32)

    half = h_ref.shape[0] // 2
    row_halves = (slice(0, half), slice(half, 2 * half))

    def head_cols(head):
        return slice(head * hd, (head + 1) * hd)

    def silu(z):
        return z * jax.nn.sigmoid(z)

    @pl.when(p == 0)
    def _():
        for rows in row_halves:
            h = _rms_norm_f32(x_ref[rows], g_ref[...]).astype(h_ref.dtype)
            h_ref[rows] = h
            hc_ref[rows] = jnp.dot(h, w_ref[:, :seg], preferred_element_type=f32).astype(hc_ref.dtype)
        bb_ref[...] = project(1).astype(bb_ref.dtype)

    @pl.when(p == 1)
    def _():
        cw = cw_ref[...]
        prev = jnp.where(i % tiles_per_seq == 0, 0.0, halo_ref[...])
        for rows in row_halves:
            u = project(0, rows) * hc_ref[rows].astype(f32)
            u_ext = jnp.concatenate([prev, u], axis=0)
            u_m1 = pltpu.roll(u_ext, 1, 0)[SUBLANES:]
            u_m2 = pltpu.roll(u_ext, 2, 0)[SUBLANES:]
            conv = u_m2 * cw[0:1] + u_m1 * cw[1:2]
            conv = conv + u * cw[2:3]
            yp_ref[rows] = _rms_norm_f32(bb_ref[rows].astype(f32) * conv, cg_ref[...])
            prev = u[-SUBLANES:]
        halo_ref[...] = prev
        for rows in row_halves:
            o_ref[0, rows] = (yp_ref[rows] * silu(project(1, rows))).astype(o_ref.dtype)

    @pl.when(p == 2)
    def _():
        acc = project(0)
        for head in range(N_HEADS):
            q_t = acc[:, head_cols(head)].T
            rs = lax.rsqrt(jnp.mean(q_t * q_t, axis=0, keepdims=True) + EPS)
            qt_ref[head] = (q_t * rs * qg_ref[...]).astype(qt_ref.dtype)
        for rows in row_halves:
            acc = project(1, rows)
            blocks = slice(rows.start // MOBA_BLOCK, rows.stop // MOBA_BLOCK)
            for head in range(N_HEADS):
                cols = head_cols(head)
                normed = _rms_norm_f32(acc[:, cols], kg_ref[:, cols])
                o_ref[0, rows, cols] = normed.astype(o_ref.dtype)
                km_ref[0, blocks, cols] = jnp.mean(normed.reshape(-1, MOBA_BLOCK, hd), axis=1)

    @pl.when(p == 3)
    def _():
        acc = project(0)
        for head in range(N_HEADS):
            vt_ref[head] = acc[:, head_cols(head)].T.astype(vt_ref.dtype)
        for rows in row_halves:
            o_ref[0, rows] = silu(project(1, rows)).astype(o_ref.dtype)


def _inproj(x2d, gain, w_bf16, q_gain_col, k_gain_row, w_out, conv_w, conv_gain, seq):
    assert IN_TILE_M % MOBA_BLOCK == 0
    m, d = x2d.shape
    seg = w_bf16.shape[1] // N_SEG
    hd = seg // N_HEADS
    tm = IN_TILE_M
    n_tiles = m // tm
    n_steps = N_SEG // 2
    wo_rows = w_out.shape[0] // (n_tiles * n_steps)
    assert wo_rows * n_tiles * n_steps == w_out.shape[0] and wo_rows % (2 * SUBLANES) == 0
    assert seq % tm == 0

    def slab(i, p):
        return (jnp.maximum(p - 1, 0), i, 0)

    def wo_block(i, p):
        return (i * n_steps + p, 0)

    def whole(shape):
        return pl.BlockSpec(shape, lambda i, p: (0,) * len(shape))

    heads_t = pl.BlockSpec((N_HEADS, hd, tm), lambda i, p: (0, 0, i))
    return pl.pallas_call(
        functools.partial(_inproj_kernel, seq // tm),
        grid=(n_tiles, n_steps),
        in_specs=[
            pl.BlockSpec((tm, d), lambda i, p: (i, 0)),
            whole((1, d)),
            pl.BlockSpec((d, 2 * seg), lambda i, p: (0, p)),
            whole((1, seg)),
            pl.BlockSpec((wo_rows, w_out.shape[1]), wo_block),
            whole((CONV_WIDTH, seg)), whole((1, seg)), whole((hd, 1)),
        ],
        out_specs=[
            pl.BlockSpec((1, tm, seg), slab),
            heads_t, heads_t,
            pl.BlockSpec((wo_rows, w_out.shape[1]), wo_block),
            pl.BlockSpec((1, tm // MOBA_BLOCK, seg), lambda i, p: (i, 0, 0)),
        ],
        out_shape=[
            jax.ShapeDtypeStruct((N_SLABS, m, seg), jnp.bfloat16),
            jax.ShapeDtypeStruct((N_HEADS, hd, m), jnp.bfloat16),
            jax.ShapeDtypeStruct((N_HEADS, hd, m), jnp.bfloat16),
            jax.ShapeDtypeStruct(w_out.shape, jnp.bfloat16),
            jax.ShapeDtypeStruct((n_tiles, tm // MOBA_BLOCK, seg), jnp.float32),
        ],
        scratch_shapes=[
            pltpu.VMEM((tm, d), jnp.bfloat16),
            pltpu.VMEM((tm, seg), jnp.bfloat16),
            pltpu.VMEM((tm, seg), jnp.bfloat16),
            pltpu.VMEM((tm, seg), jnp.float32),
            pltpu.VMEM((SUBLANES, seg), jnp.float32),
        ],
        compiler_params=pltpu.CompilerParams(
            dimension_semantics=("arbitrary", "arbitrary"),
            vmem_limit_bytes=V7X_VMEM_LIMIT_BYTES,
        ),
        name="inproj",
    )(x2d, gain.reshape(1, d), w_bf16, k_gain_row, w_out, conv_w, conv_gain.reshape(1, seg),
      q_gain_col)


def _moba_kernel(par_ref, qt_ref, k_ref, vt_ref, km_ref, o_ref):
    f32, bf16 = jnp.float32, jnp.bfloat16
    n_heads, hd, seq = qt_ref.shape
    blk = MOBA_BLOCK
    n_blocks = seq // blk
    assert n_blocks == SUBLANES and hd == LANES

    key_row = lax.broadcasted_iota(jnp.int32, (blk, blk), 0)
    qry_col = lax.broadcasted_iota(jnp.int32, (blk, blk), 1)
    causal = key_row <= qry_col
    blk_row = lax.broadcasted_iota(jnp.int32, (n_blocks, blk), 0)

    def head_cols(head):
        return slice(head * hd, (head + 1) * hd)

    def keys(head, n):
        return k_ref[0, 0, n * blk:(n + 1) * blk, head_cols(head)]

    def block_means(head):
        k_mean = km_ref[0, :, head_cols(head)]
        km_hi = k_mean.astype(bf16)
        km_lo = (k_mean - km_hi.astype(f32)).astype(bf16)
        return jnp.concatenate([km_hi, km_lo], axis=0)

    km_hl = [block_means(head) for head in range(n_heads)]

    def block_bias(head, own, q_t):
        if own <= MOBA_TOPK:
            return None
        g2 = jnp.dot(km_hl[head], q_t, preferred_element_type=f32)
        gate = g2[:n_blocks] + g2[n_blocks:]
        rank = jnp.zeros(gate.shape, f32)
        for m in range(own):
            g_m = gate[m:m + 1, :]
            ahead = (g_m > gate) | ((g_m == gate) & (blk_row > m))
            rank = rank + jnp.where(ahead, 1.0, 0.0)
        return jnp.where((rank < MOBA_TOPK) | (blk_row >= own), 0.0, MASKED)

    def finish(head, own, probs, l):
        n_keys = (own + 1) * blk
        p_all = jnp.concatenate([p.astype(bf16) for p in probs], axis=0)
        acc = jnp.dot(vt_ref[head, :, 0:n_keys], p_all, preferred_element_type=f32)
        out_t = acc / l
        o_ref[0, own * blk:(own + 1) * blk, head_cols(head)] = out_t.T.astype(o_ref.dtype)

    def tile_exact(head, own):
        q_t = qt_ref[head, :, own * blk:(own + 1) * blk]
        bias = block_bias(head, own, q_t)
        parts = []
        for n in range(own + 1):
            part = jnp.dot(keys(head, n), q_t, preferred_element_type=f32)
            if n == own:
                part = jnp.where(causal, part, MASKED)
            elif bias is not None:
                part = part + bias[n:n + 1, :]
            parts.append(part)
        m = functools.reduce(jnp.maximum, [jnp.max(part, axis=0, keepdims=True) for part in parts])
        probs = [jnp.exp2(part - m) for part in parts]
        l = functools.reduce(jnp.add, [jnp.sum(p, axis=0, keepdims=True) for p in probs])
        finish(head, own, probs, l)

    def tile_bounded(head, own):
        q_t = qt_ref[head, :, own * blk:(own + 1) * blk]
        q_f = q_t.astype(f32)
        bound = jnp.sqrt(jnp.sum(q_f * q_f, axis=0, keepdims=True)) * par_ref[0]
        bias = block_bias(head, own, q_t)
        if bias is None:
            bias = jnp.zeros((n_blocks, blk), f32)
        shift = jnp.broadcast_to(-bound, (SUBLANES, blk))
        pad = jnp.zeros((hd - n_blocks - SUBLANES, blk), f32)
        extra = jnp.concatenate([bias, shift, pad], axis=0).astype(bf16)
        rhs = jnp.concatenate([q_t, extra], axis=0)
        lane = lax.broadcasted_iota(jnp.int32, (blk, hd), 1)
        probs = []
        for n in range(own + 1):
            pick = jnp.where((lane == n) | (lane == n_blocks), 1.0, 0.0).astype(bf16)
            lhs = jnp.concatenate([keys(head, n), pick], axis=1)
            part = jnp.dot(lhs, rhs, preferred_element_type=f32)
            if n == own:
                part = jnp.where(causal, part, MASKED)
            probs.append(jnp.exp2(part))
        l = functools.reduce(jnp.add, [jnp.sum(p, axis=0, keepdims=True) for p in probs])
        finish(head, own, probs, l)

    bounded_ok = par_ref[1] > 0.5

    @pl.when(bounded_ok)
    def _():
        for own in range(n_blocks):
            for head in range(n_heads):
                tile_bounded(head, own)

    @pl.when(jnp.logical_not(bounded_ok))
    def _():
        for own in range(n_blocks):
            for head in range(n_heads):
                tile_exact(head, own)


def _moba(proj4d, q_t, v_t, k_means, params):
    _, bsz, seq, seg = proj4d.shape
    hd = seg // N_HEADS
    grp = MOBA_HEADS_PER_STEP
    heads_t = pl.BlockSpec((grp, hd, seq), lambda b, g: (g, 0, b))
    return pl.pallas_call(
        _moba_kernel,
        grid=(bsz, N_HEADS // grp),
        in_specs=[pl.BlockSpec(memory_space=pltpu.SMEM), heads_t,
                  pl.BlockSpec((1, 1, seq, grp * hd), lambda b, g: (SLAB_K, b, 0, g)), heads_t,
                  pl.BlockSpec((1, seq // MOBA_BLOCK, grp * hd), lambda b, g: (b, 0, g))],
        out_specs=pl.BlockSpec((1, seq, grp * hd), lambda b, g: (b, 0, g)),
        out_shape=jax.ShapeDtypeStruct((bsz, seq, seg), jnp.bfloat16),
        compiler_params=pltpu.CompilerParams(
            dimension_semantics=("arbitrary", "arbitrary"),
            vmem_limit_bytes=V7X_VMEM_LIMIT_BYTES,
        ),
        name="moba",
    )(params, q_t, proj4d, v_t, k_means)


def _outproj_kernel(yc_ref, ga_ref, a_ref, x_ref, ag_ref, w_ref, o_ref):
    f32 = jnp.float32
    half = o_ref.shape[0] // 2
    for rows in (slice(0, half), slice(half, 2 * half)):
        y_attn = _rms_norm_f32(a_ref[rows].astype(f32), ag_ref[...]) * ga_ref[0, rows].astype(f32)
        y = jnp.concatenate([yc_ref[0, rows], y_attn.astype(jnp.bfloat16)], axis=1)
        o_ref[rows] = x_ref[rows] + jnp.dot(y, w_ref[...], preferred_element_type=f32)


def _outproj(slabs, attn2d, x2d, attn_gain, w_out_bf16):
    _, m, seg = slabs.shape
    d = x2d.shape[1]
    tm = OUT_TILE_M

    def slab_spec(slab):
        return pl.BlockSpec((1, tm, seg), lambda i: (slab, i, 0))

    def whole(shape):
        return pl.BlockSpec(shape, lambda i: (0,) * len(shape))

    return pl.pallas_call(
        _outproj_kernel,
        grid=(m // tm,),
        in_specs=[
            slab_spec(SLAB_CONV), slab_spec(SLAB_GATE),
            pl.BlockSpec((tm, seg), lambda i: (i, 0)),
            pl.BlockSpec((tm, d), lambda i: (i, 0)),
            whole((1, seg)),
            pl.BlockSpec(w_out_bf16.shape, lambda i: (0, 0), pipeline_mode=pl.Buffered(1)),
        ],
        out_specs=pl.BlockSpec((tm, d), lambda i: (i, 0)),
        out_shape=jax.ShapeDtypeStruct((m, d), jnp.float32),
        compiler_params=pltpu.CompilerParams(
            dimension_semantics=("arbitrary",),
            vmem_limit_bytes=V7X_VMEM_LIMIT_BYTES,
        ),
        name="outproj",
    )(slabs, slabs, attn2d, x2d, attn_gain.reshape(1, seg), w_out_bf16)


def _head_gains(q_gain, k_gain):
    hd = q_gain.shape[0]
    score_scale = hd ** -0.5 * LOG2E
    return (q_gain * score_scale).reshape(hd, 1), jnp.tile(k_gain, N_HEADS).reshape(1, N_HEADS * hd)


def _softmax_shift_params(q_gain, k_gain):
    hd = q_gain.shape[0]
    k_max = math.sqrt(hd) * jnp.max(jnp.abs(k_gain)) * BF16_NORM_SLACK
    q_max = math.sqrt(hd) * jnp.max(jnp.abs(q_gain)) * (hd ** -0.5 * LOG2E) * BF16_NORM_SLACK
    bounded_ok = 2.0 * q_max * k_max <= MAX_SHIFT_GAP
    return jnp.stack([k_max, bounded_ok.astype(jnp.float32)]).astype(jnp.float32)


def kernel(x, norm_gain, w_in, conv_w, q_norm_gain, k_norm_gain, conv_out_gain, attn_out_gain, w_out):
    bsz, seq, d = x.shape
    depth = norm_gain.shape[0]
    seg = w_in.shape[2] // N_SEG
    assert seq % MOBA_BLOCK == 0 and (bsz * seq) % OUT_TILE_M == 0 and (bsz * seq) % IN_TILE_M == 0
    assert w_out.shape[1] == 2 * seg and seg == N_HEADS * q_norm_gain.shape[1]
    x2d = x.reshape(bsz * seq, d)
    for layer in range(depth):
        slabs, q_t, v_t, w_out_bf16, k_means = _inproj(
            x2d, norm_gain[layer], w_in[layer].astype(jnp.bfloat16),
            *_head_gains(q_norm_gain[layer], k_norm_gain[layer]), w_out[layer],
            conv_w[layer], conv_out_gain[layer], seq)
        attn = _moba(slabs.reshape(N_SLABS, bsz, seq, seg), q_t, v_t,
                     k_means.reshape(bsz, seq // MOBA_BLOCK, seg),
                     _softmax_shift_params(q_norm_gain[layer], k_norm_gain[layer]))
        x2d = _outproj(slabs, attn.reshape(bsz * seq, seg), x2d, attn_out_gain[layer], w_out_bf16)
    return x2d.reshape(bsz, seq, d)
```

```python
import functools
import math

import jax
import jax.numpy as jnp
from jax import lax
from jax.experimental import pallas as pl
from jax.experimental.pallas import tpu as pltpu

N_HEADS = 8
CONV_WIDTH = 3
MOBA_BLOCK = 256
MOBA_TOPK = 3
EPS = 1e-6
N_SEG = 8
N_SLABS = 3
SLAB_CONV, SLAB_K, SLAB_GATE = range(N_SLABS)

MASKED = -1e30
LOG2E = math.log2(math.e)
MAX_SHIFT_GAP = 100.0
BF16_NORM_SLACK = 1.0 + 2.0 ** -6

V7X_VMEM_LIMIT_BYTES = 60 * 1024 * 1024
SUBLANES = 8
LANES = 128

IN_TILE_M = 1024
OUT_TILE_M = 1024
MOBA_HEADS_PER_STEP = 2


def _rms_norm_f32(x, gain):
    return x * lax.rsqrt(jnp.mean(x * x, axis=-1, keepdims=True) + EPS) * gain


def _inproj_kernel(tiles_per_seq, x_ref, g_ref, w_ref, kg_ref, wo_ref, cw_ref, cg_ref, qg_ref,
                   o_ref, qt_ref, vt_ref, wob_ref, km_ref,
                   h_ref, hc_ref, bb_ref, yp_ref, halo_ref):
    i, p = pl.program_id(0), pl.program_id(1)
    f32 = jnp.float32
    hd = qt_ref.shape[1]
    seg = w_ref.shape[1] // 2

    wob_ref[...] = wo_ref[...].astype(wob_ref.dtype)

    def project(which, rows=slice(None)):
        w = w_ref[:, which * seg:(which + 1) * seg]
        return jnp.dot(h_ref[rows], w, preferred_element_type=f32)

    half = h_ref.shape[0] // 2
    row_halves = (slice(0, half), slice(half, 2 * half))

    def head_cols(head):
        return slice(head * hd, (head + 1) * hd)

    def silu(z):
        return z * jax.nn.sigmoid(z)

    @pl.when(p == 0)
    def _():
        for rows in row_halves:
            h = _rms_norm_f32(x_ref[rows], g_ref[...]).astype(h_ref.dtype)
            h_ref[rows] = h
            hc_ref[rows] = jnp.dot(h, w_ref[:, :seg], preferred_element_type=f32).astype(hc_ref.dtype)
        bb_ref[...] = project(1).astype(bb_ref.dtype)

    @pl.when(p == 1)
    def _():
        cw = cw_ref[...]
        prev = jnp.where(i % tiles_per_seq == 0, 0.0, halo_ref[...])
        for rows in row_halves:
            u = project(0, rows) * hc_ref[rows].astype(f32)
            u_ext = jnp.concatenate([prev, u], axis=0)
            u_m1 = pltpu.roll(u_ext, 1, 0)[SUBLANES:]
            u_m2 = pltpu.roll(u_ext, 2, 0)[SUBLANES:]
            conv = u_m2 * cw[0:1] + u_m1 * cw[1:2]
            conv = conv + u * cw[2:3]
            yp_ref[rows] = _rms_norm_f32(bb_ref[rows].astype(f32) * conv, cg_ref[...])
            prev = u[-SUBLANES:]
        halo_ref[...] = prev
        for rows in row_halves:
            o_ref[0, rows] = (yp_ref[rows] * silu(project(1, rows))).astype(o_ref.dtype)

    @pl.when(p == 2)
    def _():
        acc = project(0)
        for head in range(N_HEADS):
            q_t = acc[:, head_cols(head)].T
            rs = lax.rsqrt(jnp.mean(q_t * q_t, axis=0, keepdims=True) + EPS)
            qt_ref[head] = (q_t * rs * qg_ref[...]).astype(qt_ref.dtype)
        for rows in row_halves:
            acc = project(1, rows)
            blocks = slice(rows.start // MOBA_BLOCK, rows.stop // MOBA_BLOCK)
            for head in range(N_HEADS):
                cols = head_cols(head)
                normed = _rms_norm_f32(acc[:, cols], kg_ref[:, cols])
                o_ref[0, rows, cols] = normed.astype(o_ref.dtype)
                km_ref[0, blocks, cols] = jnp.mean(normed.reshape(-1, MOBA_BLOCK, hd), axis=1)

    @pl.when(p == 3)
    def _():
        acc = project(0)
        for head in range(N_HEADS):
            vt_ref[head] = acc[:, head_cols(head)].T.astype(vt_ref.dtype)
        for rows in row_halves:
            o_ref[0, rows] = silu(project(1, rows)).astype(o_ref.dtype)


def _inproj(x2d, gain, w_bf16, q_gain_col, k_gain_row, w_out, conv_w, conv_gain, seq):
    assert IN_TILE_M % MOBA_BLOCK == 0
    m, d = x2d.shape
    seg = w_bf16.shape[1] // N_SEG
    hd = seg // N_HEADS
    tm = IN_TILE_M
    n_tiles = m // tm
    n_steps = N_SEG // 2
    wo_rows = w_out.shape[0] // (n_tiles * n_steps)
    assert wo_rows * n_tiles * n_steps == w_out.shape[0] and wo_rows % (2 * SUBLANES) == 0
    assert seq % tm == 0

    def slab(i, p):
        return (jnp.maximum(p - 1, 0), i, 0)

    def wo_block(i, p):
        return (i * n_steps + p, 0)

    def whole(shape):
        return pl.BlockSpec(shape, lambda i, p: (0,) * len(shape))

    heads_t = pl.BlockSpec((N_HEADS, hd, tm), lambda i, p: (0, 0, i))
    return pl.pallas_call(
        functools.partial(_inproj_kernel, seq // tm),
        grid=(n_tiles, n_steps),
        in_specs=[
            pl.BlockSpec((tm, d), lambda i, p: (i, 0)),
            whole((1, d)),
            pl.BlockSpec((d, 2 * seg), lambda i, p: (0, p)),
            whole((1, seg)),
            pl.BlockSpec((wo_rows, w_out.shape[1]), wo_block),
            whole((CONV_WIDTH, seg)), whole((1, seg)), whole((hd, 1)),
        ],
        out_specs=[
            pl.BlockSpec((1, tm, seg), slab),
            heads_t, heads_t,
            pl.BlockSpec((wo_rows, w_out.shape[1]), wo_block),
            pl.BlockSpec((1, tm // MOBA_BLOCK, seg), lambda i, p: (i, 0, 0)),
        ],
        out_shape=[
            jax.ShapeDtypeStruct((N_SLABS, m, seg), jnp.bfloat16),
            jax.ShapeDtypeStruct((N_HEADS, hd, m), jnp.bfloat16),
            jax.ShapeDtypeStruct((N_HEADS, hd, m), jnp.bfloat16),
            jax.ShapeDtypeStruct(w_out.shape, jnp.bfloat16),
            jax.ShapeDtypeStruct((n_tiles, tm // MOBA_BLOCK, seg), jnp.float32),
        ],
        scratch_shapes=[
            pltpu.VMEM((tm, d), jnp.bfloat16),
            pltpu.VMEM((tm, seg), jnp.bfloat16),
            pltpu.VMEM((tm, seg), jnp.bfloat16),
            pltpu.VMEM((tm, seg), jnp.float32),
            pltpu.VMEM((SUBLANES, seg), jnp.float32),
        ],
        compiler_params=pltpu.CompilerParams(
            dimension_semantics=("arbitrary", "arbitrary"),
            vmem_limit_bytes=V7X_VMEM_LIMIT_BYTES,
        ),
        name="inproj",
    )(x2d, gain.reshape(1, d), w_bf16, k_gain_row, w_out, conv_w, conv_gain.reshape(1, seg),
      q_gain_col)


def _moba_kernel(par_ref, qt_ref, k_ref, vt_ref, km_ref, o_ref):
    f32, bf16 = jnp.float32, jnp.bfloat16
    n_heads, hd, seq = qt_ref.shape
    blk = MOBA_BLOCK
    n_blocks = seq // blk
    assert n_blocks == SUBLANES and hd == LANES

    key_row = lax.broadcasted_iota(jnp.int32, (blk, blk), 0)
    qry_col = lax.broadcasted_iota(jnp.int32, (blk, blk), 1)
    causal = key_row <= qry_col
    blk_row = lax.broadcasted_iota(jnp.int32, (n_blocks, blk), 0)

    def head_cols(head):
        return slice(head * hd, (head + 1) * hd)

    def keys(head, n):
        return k_ref[0, 0, n * blk:(n + 1) * blk, head_cols(head)]

    def block_means(head):
        k_mean = km_ref[0, :, head_cols(head)]
        km_hi = k_mean.astype(bf16)
        km_lo = (k_mean - km_hi.astype(f32)).astype(bf16)
        return jnp.concatenate([km_hi, km_lo], axis=0)

    km_hl = [block_means(head) for head in range(n_heads)]

    def block_bias(head, own, q_t):
        if own <= MOBA_TOPK:
            return None
        g2 = jnp.dot(km_hl[head], q_t, preferred_element_type=f32)
        gate = g2[:n_blocks] + g2[n_blocks:]
        rank = jnp.zeros(gate.shape, f32)
        for m in range(own):
            g_m = gate[m:m + 1, :]
            ahead = (g_m > gate) | ((g_m == gate) & (blk_row > m))
            rank = rank + jnp.where(ahead, 1.0, 0.0)
        return jnp.where((rank < MOBA_TOPK) | (blk_row >= own), 0.0, MASKED)

    def finish(head, own, probs, l):
        n_keys = (own + 1) * blk
        p_all = jnp.concatenate([p.astype(bf16) for p in probs], axis=0)
        acc = jnp.dot(vt_ref[head, :, 0:n_keys], p_all, preferred_element_type=f32)
        out_t = acc / l
        o_ref[0, own * blk:(own + 1) * blk, head_cols(head)] = out_t.T.astype(o_ref.dtype)

    def tile_exact(head, own):
        q_t = qt_ref[head, :, own * blk:(own + 1) * blk]
        bias = block_bias(head, own, q_t)
        parts = []
        for n in range(own + 1):
            part = jnp.dot(keys(head, n), q_t, preferred_element_type=f32)
            if n == own:
                part = jnp.where(causal, part, MASKED)
            elif bias is not None:
                part = part + bias[n:n + 1, :]
            parts.append(part)
        m = functools.reduce(jnp.maximum, [jnp.max(part, axis=0, keepdims=True) for part in parts])
        probs = [jnp.exp2(part - m) for part in parts]
        l = functools.reduce(jnp.add, [jnp.sum(p, axis=0, keepdims=True) for p in probs])
        finish(head, own, probs, l)

    def tile_bounded(head, own):
        q_t = qt_ref[head, :, own * blk:(own + 1) * blk]
        q_f = q_t.astype(f32)
        bound = jnp.sqrt(jnp.sum(q_f * q_f, axis=0, keepdims=True)) * par_ref[0]
        bias = block_bias(head, own, q_t)
        if bias is None:
            bias = jnp.zeros((n_blocks, blk), f32)
        shift = jnp.broadcast_to(-bound, (SUBLANES, blk))
        pad = jnp.zeros((hd - n_blocks - SUBLANES, blk), f32)
        extra = jnp.concatenate([bias, shift, pad], axis=0).astype(bf16)
        rhs = jnp.concatenate([q_t, extra], axis=0)
        lane = lax.broadcasted_iota(jnp.int32, (blk, hd), 1)
        probs = []
        for n in range(own + 1):
            pick = jnp.where((lane == n) | (lane == n_blocks), 1.0, 0.0).astype(bf16)
            lhs = jnp.concatenate([keys(head, n), pick], axis=1)
            part = jnp.dot(lhs, rhs, preferred_element_type=f32)
            if n == own:
                part = jnp.where(causal, part, MASKED)
            probs.append(jnp.exp2(part))
        l = functools.reduce(jnp.add, [jnp.sum(p, axis=0, keepdims=True) for p in probs])
        finish(head, own, probs, l)

    bounded_ok = par_ref[1] > 0.5

    @pl.when(bounded_ok)
    def _():
        for own in range(n_blocks):
            for head in range(n_heads):
                tile_bounded(head, own)

    @pl.when(jnp.logical_not(bounded_ok))
    def _():
        for own in range(n_blocks):
            for head in range(n_heads):
                tile_exact(head, own)


def _moba(proj4d, q_t, v_t, k_means, params):
    _, bsz, seq, seg = proj4d.shape
    hd = seg // N_HEADS
    grp = MOBA_HEADS_PER_STEP
    heads_t = pl.BlockSpec((grp, hd, seq), lambda b, g: (g, 0, b))
    return pl.pallas_call(
        _moba_kernel,
        grid=(bsz, N_HEADS // grp),
        in_specs=[pl.BlockSpec(memory_space=pltpu.SMEM), heads_t,
                  pl.BlockSpec((1, 1, seq, grp * hd), lambda b, g: (SLAB_K, b, 0, g)), heads_t,
                  pl.BlockSpec((1, seq // MOBA_BLOCK, grp * hd), lambda b, g: (b, 0, g))],
        out_specs=pl.BlockSpec((1, seq, grp * hd), lambda b, g: (b, 0, g)),
        out_shape=jax.ShapeDtypeStruct((bsz, seq, seg), jnp.bfloat16),
        compiler_params=pltpu.CompilerParams(
            dimension_semantics=("arbitrary", "arbitrary"),
            vmem_limit_bytes=V7X_VMEM_LIMIT_BYTES,
        ),
        name="moba",
    )(params, q_t, proj4d, v_t, k_means)


def _outproj_kernel(yc_ref, ga_ref, a_ref, x_ref, ag_ref, w_ref, o_ref):
    f32 = jnp.float32
    half = o_ref.shape[0] // 2
    for rows in (slice(0, half), slice(half, 2 * half)):
        y_attn = _rms_norm_f32(a_ref[rows].astype(f32), ag_ref[...]) * ga_ref[0, rows].astype(f32)
        y = jnp.concatenate([yc_ref[0, rows], y_attn.astype(jnp.bfloat16)], axis=1)
        o_ref[rows] = x_ref[rows] + jnp.dot(y, w_ref[...], preferred_element_type=f32)


def _outproj(slabs, attn2d, x2d, attn_gain, w_out_bf16):
    _, m, seg = slabs.shape
    d = x2d.shape[1]
    tm = OUT_TILE_M

    def slab_spec(slab):
        return pl.BlockSpec((1, tm, seg), lambda i: (slab, i, 0))

    def whole(shape):
        return pl.BlockSpec(shape, lambda i: (0,) * len(shape))

    return pl.pallas_call(
        _outproj_kernel,
        grid=(m // tm,),
        in_specs=[
            slab_spec(SLAB_CONV), slab_spec(SLAB_GATE),
            pl.BlockSpec((tm, seg), lambda i: (i, 0)),
            pl.BlockSpec((tm, d), lambda i: (i, 0)),
            whole((1, seg)),
            pl.BlockSpec(w_out_bf16.shape, lambda i: (0, 0), pipeline_mode=pl.Buffered(1)),
        ],
        out_specs=pl.BlockSpec((tm, d), lambda i: (i, 0)),
        out_shape=jax.ShapeDtypeStruct((m, d), jnp.float32),
        compiler_params=pltpu.CompilerParams(
            dimension_semantics=("arbitrary",),
            vmem_limit_bytes=V7X_VMEM_LIMIT_BYTES,
        ),
        name="outproj",
    )(slabs, slabs, attn2d, x2d, attn_gain.reshape(1, seg), w_out_bf16)


def _head_gains(q_gain, k_gain):
    hd = q_gain.shape[0]
    score_scale = hd ** -0.5 * LOG2E
    return (q_gain * score_scale).reshape(hd, 1), jnp.tile(k_gain, N_HEADS).reshape(1, N_HEADS * hd)


def _softmax_shift_params(q_gain, k_gain):
    hd = q_gain.shape[0]
    k_max = math.sqrt(hd) * jnp.max(jnp.abs(k_gain)) * BF16_NORM_SLACK
    q_max = math.sqrt(hd) * jnp.max(jnp.abs(q_gain)) * (hd ** -0.5 * LOG2E) * BF16_NORM_SLACK
    bounded_ok = 2.0 * q_max * k_max <= MAX_SHIFT_GAP
    return jnp.stack([k_max, bounded_ok.astype(jnp.float32)]).astype(jnp.float32)


def kernel(x, norm_gain, w_in, conv_w, q_norm_gain, k_norm_gain, conv_out_gain, attn_out_gain, w_out):
    bsz, seq, d = x.shape
    depth = norm_gain.shape[0]
    seg = w_in.shape[2] // N_SEG
    assert seq % MOBA_BLOCK == 0 and (bsz * seq) % OUT_TILE_M == 0 and (bsz * seq) % IN_TILE_M == 0
    assert w_out.shape[1] == 2 * seg and seg == N_HEADS * q_norm_gain.shape[1]
    x2d = x.reshape(bsz * seq, d)
    for layer in range(depth):
        slabs, q_t, v_t, w_out_bf16, k_means = _inproj(
            x2d, norm_gain[layer], w_in[layer].astype(jnp.bfloat16),
            *_head_gains(q_norm_gain[layer], k_norm_gain[layer]), w_out[layer],
            conv_w[layer], conv_out_gain[layer], seq)
        attn = _moba(slabs.reshape(N_SLABS, bsz, seq, seg), q_t, v_t,
                     k_means.reshape(bsz, seq // MOBA_BLOCK, seg),
                     _softmax_shift_params(q_norm_gain[layer], k_norm_gain[layer]))
        x2d = _outproj(slabs, attn.reshape(bsz * seq, seg), x2d, attn_out_gain[layer], w_out_bf16)
    return x2d.reshape(bsz, seq, d)
```
